```python
import math
import jax, jax.numpy as jnp
from jax import lax
import numpy as np

D_MODEL = 1024
BATCH = 32
SEQ = 2048
DEPTH = 2

GRID_W = 64
CTX_LEN = 256
EPS = 1e-6

D_MIX = D_MODEL
A_WIDTH = D_MIX // 4
A_HEADS = 4
A_HEAD_DIM = A_WIDTH // A_HEADS
CHUNK = 128
B_WIDTH = D_MIX // 2
SSM_GROUP = 16
SSM_GROUPS = B_WIDTH // SSM_GROUP
SSM_STATE = 64
C_WIDTH = D_MIX - A_WIDTH - B_WIDTH
POOL_WINDOWS = (2, 4, 8, 16)
POOL_GROUP = C_WIDTH // len(POOL_WINDOWS)
D_IN = 2 * A_WIDTH + B_WIDTH + C_WIDTH
D_FF = ((-(-8 * D_MODEL // 3) + 255) // 256) * 256

kernel_name = "hybrid_gmlp_s5_pool_dit_prefix"


def rms_norm(x, g):
    xf = x.astype(jnp.float32)
    y = xf * lax.rsqrt(jnp.mean(xf * xf, axis=-1, keepdims=True) + EPS)
    return (y * g.astype(jnp.float32)).astype(x.dtype)


def layer_norm(x):
    xf = x.astype(jnp.float32)
    mu = jnp.mean(xf, axis=-1, keepdims=True)
    var = jnp.mean(jnp.square(xf - mu), axis=-1, keepdims=True)
    return ((xf - mu) * lax.rsqrt(var + EPS)).astype(x.dtype)


def modulate(h, shift, scale):
    return h * (1 + scale) + shift


def sincos_2d(rows, cols, dim):
    quarter = dim // 4
    omega = 1.0 / (10000.0 ** (jnp.arange(quarter, dtype=jnp.float32) / quarter))
    r = jnp.arange(rows, dtype=jnp.float32)[:, None] * omega
    cc = jnp.arange(cols, dtype=jnp.float32)[:, None] * omega
    er = jnp.concatenate([jnp.sin(r), jnp.cos(r)], axis=-1)
    ec = jnp.concatenate([jnp.sin(cc), jnp.cos(cc)], axis=-1)
    pe = jnp.concatenate([jnp.broadcast_to(er[:, None, :], (rows, cols, dim // 2)),
                          jnp.broadcast_to(ec[None, :, :], (rows, cols, dim // 2))], axis=-1)
    return pe.reshape(rows * cols, dim)


def spatial_gating(z, w_s, b_s):
    bsz, n, _ = z.shape
    z = jax.nn.gelu(z)
    u, v = jnp.split(z, 2, axis=-1)
    v = layer_norm(v.reshape(bsz, n // CHUNK, CHUNK, A_HEADS, A_HEAD_DIM))
    s = jnp.einsum('hpq,bkqhd->bkphd', w_s, v) + b_s.T[None, None, :, :, None]
    return u * s.reshape(bsz, n, A_WIDTH)


def ssm_discretize(lam_re, lam_im, log_dt, b_re, b_im):
    lam = lax.complex(lam_re.astype(jnp.float32), lam_im.astype(jnp.float32))
    dt = jnp.exp(log_dt.astype(jnp.float32))[:, None]
    lam_bar = jnp.exp(lam * dt)
    b = lax.complex(b_re.astype(jnp.float32), b_im.astype(jnp.float32))
    b_bar = ((lam_bar - 1.0) / lam)[..., None] * b
    return lam_bar, b_bar


def diag_scan(lam_bar, bu, h0, reverse):
    if h0 is not None:
        edge = bu.shape[1] - 1 if reverse else 0
        bu = bu.at[:, edge].add(lam_bar * h0)
    a = jnp.broadcast_to(lam_bar, bu.shape)

    def combine(e1, e2):
        a1, b1 = e1
        a2, b2 = e2
        return a1 * a2, a2 * b1 + b2

    _, h = lax.associative_scan(combine, (a, bu), reverse=reverse, axis=1)
    return h


def ssm_mixer(u_lat, u_ctx, lam_re, lam_im, log_dt, b_re, b_im, c_re, c_im, d, glu_w, glu_b, need_ctx):
    def groups(u):
        return u.reshape(u.shape[0], u.shape[1], SSM_GROUPS, SSM_GROUP).astype(jnp.float32)

    g_lat, g_ctx = groups(u_lat), groups(u_ctx)
    df = d.astype(jnp.float32)
    y_lat = df * g_lat
    y_ctx = df * g_ctx if need_ctx else None
    for k, reverse in enumerate((False, True)):
        lam_bar, b_bar = ssm_discretize(lam_re[k], lam_im[k], log_dt[k], b_re[k], b_im[k])
        cm = lax.complex(c_re[k].astype(jnp.float32), c_im[k].astype(jnp.float32))
        bu_ctx = jnp.einsum('blgh,gph->blgp', g_ctx.astype(jnp.complex64), b_bar)
        h_ctx = diag_scan(lam_bar, bu_ctx, None, reverse)
        h_end = h_ctx[:, 0] if reverse else h_ctx[:, -1]
        bu_lat = jnp.einsum('blgh,gph->blgp', g_lat.astype(jnp.complex64), b_bar)
        h_lat = diag_scan(lam_bar, bu_lat, h_end, reverse)
        y_lat = y_lat + jnp.einsum('ghp,blgp->blgh', cm, h_lat).real
        if need_ctx:
            y_ctx = y_ctx + jnp.einsum('ghp,blgp->blgh', cm, h_ctx).real

    def glu(y, dtype):
        g = jax.nn.gelu(y.reshape(y.shape[0], y.shape[1], B_WIDTH)).astype(dtype)
        return g * jax.nn.sigmoid(g @ glu_w + glu_b)

    out_lat = glu(y_lat, u_lat.dtype)
    out_ctx = glu(y_ctx, u_ctx.dtype) if need_ctx else None
    return out_lat, out_ctx


def window_mean(x, w):
    n = x.shape[-2]
    cs = jnp.cumsum(x.astype(jnp.float32), axis=-2)
    cs = jnp.concatenate([jnp.zeros_like(cs[..., :1, :]), cs], axis=-2)
    t = np.arange(n)
    lo = np.clip(t - w // 2, 0, n)
    hi = np.clip(t - w // 2 + w, 0, n)
    cnt = (hi - lo).astype(np.float32)[:, None]
    return ((jnp.take(cs, hi, axis=-2) - jnp.take(cs, lo, axis=-2)) / cnt).astype(x.dtype)


def pool_mixer(p, pool_w, pool_scale, rows):
    bsz, n, _ = p.shape
    outs = []
    for i, w in enumerate(POOL_WINDOWS):
        pg = p[..., i * POOL_GROUP:(i + 1) * POOL_GROUP]
        if rows is None:
            m = window_mean(pg, w)
        else:
            m = window_mean(pg.reshape(bsz, rows, GRID_W, POOL_GROUP), w).reshape(bsz, n, POOL_GROUP)
        outs.append((m - pg) @ pool_w[i])
    return jnp.concatenate(outs, axis=-1) * pool_scale


def mixing_sublayer(h_lat, h_ctx, rows, need_ctx, w_in, w_out, sgu_w, sgu_b,
                    lam_re, lam_im, log_dt, b_re, b_im, c_re, c_im, d, glu_w, glu_b,
                    pool_w, pool_scale):
    b_lo, b_hi = 2 * A_WIDTH, 2 * A_WIDTH + B_WIDTH
    z_lat = h_lat @ w_in
    if need_ctx:
        z_ctx = h_ctx @ w_in
        u_ctx = z_ctx[..., b_lo:b_hi]
    else:
        u_ctx = h_ctx @ w_in[:, b_lo:b_hi]
    a_lat = spatial_gating(z_lat[..., :b_lo], sgu_w, sgu_b)
    s_lat, s_ctx = ssm_mixer(z_lat[..., b_lo:b_hi], u_ctx, lam_re, lam_im, log_dt, b_re, b_im,
                             c_re, c_im, d, glu_w, glu_b, need_ctx)
    p_lat = pool_mixer(z_lat[..., b_hi:], pool_w, pool_scale, rows)
    m_lat = jnp.concatenate([a_lat, s_lat, p_lat], axis=-1) @ w_out
    m_ctx = None
    if need_ctx:
        a_ctx = spatial_gating(z_ctx[..., :b_lo], sgu_w, sgu_b)
        p_ctx = pool_mixer(z_ctx[..., b_hi:], pool_w, pool_scale, None)
        m_ctx = jnp.concatenate([a_ctx, s_ctx, p_ctx], axis=-1) @ w_out
    return m_lat, m_ctx


def swiglu(h, w_gate, w_up, w_down):
    return (jax.nn.silu(h @ w_gate) * (h @ w_up)) @ w_down


def setup_inputs(seed: int = 0) -> dict:
    key = jax.random.key(seed)
    ks = jax.random.split(key, 32)
    f32 = jnp.float32

    def nrm(k, shape, scale):
        return jax.random.normal(k, shape, f32) * scale

    lam_im0 = math.pi * jnp.arange(SSM_STATE, dtype=f32)
    return {
        "x": nrm(ks[0], (BATCH, SEQ, D_MODEL), 1.0),
        "c": nrm(ks[1], (BATCH, D_MODEL), 1.0),
        "ctx": nrm(ks[2], (BATCH, CTX_LEN, D_MODEL), 1.0),
        "c_ctx": nrm(ks[3], (D_MODEL,), 1.0),
        "w_mod": nrm(ks[4], (DEPTH, D_MODEL, 6 * D_MODEL), 0.5 * D_MODEL ** -0.5),
        "b_mod": nrm(ks[5], (DEPTH, 6 * D_MODEL), 0.02),
        "norm_mix_pre": 1.0 + nrm(ks[6], (DEPTH, D_MODEL), 0.1),
        "norm_mix_post": 1.0 + nrm(ks[7], (DEPTH, D_MODEL), 0.1),
        "norm_ffn_pre": 1.0 + nrm(ks[8], (DEPTH, D_MODEL), 0.1),
        "norm_ffn_post": 1.0 + nrm(ks[9], (DEPTH, D_MODEL), 0.1),
        "w_in": nrm(ks[10], (DEPTH, D_MODEL, D_IN), D_MODEL ** -0.5),
        "w_out": nrm(ks[11], (DEPTH, D_MIX, D_MODEL), D_MIX ** -0.5),
        "sgu_w": nrm(ks[12], (DEPTH, A_HEADS, CHUNK, CHUNK), CHUNK ** -0.5),
        "sgu_b": 1.0 + nrm(ks[13], (DEPTH, A_HEADS, CHUNK), 0.1),
        "ssm_lam_re": -0.5 + nrm(ks[14], (DEPTH, 2, SSM_GROUPS, SSM_STATE), 0.01),
        "ssm_lam_im": lam_im0 + nrm(ks[15], (DEPTH, 2, SSM_GROUPS, SSM_STATE), 0.01),
        "ssm_log_dt": jax.random.uniform(ks[16], (DEPTH, 2, SSM_GROUPS), f32,
                                         minval=math.log(1e-3), maxval=math.log(1e-1)),
        "ssm_b_re": nrm(ks[17], (DEPTH, 2, SSM_GROUPS, SSM_STATE, SSM_GROUP), (2 * SSM_GROUP) ** -0.5),
        "ssm_b_im": nrm(ks[18], (DEPTH, 2, SSM_GROUPS, SSM_STATE, SSM_GROUP), (2 * SSM_GROUP) ** -0.5),
        "ssm_c_re": nrm(ks[19], (DEPTH, 2, SSM_GROUPS, SSM_GROUP, SSM_STATE), SSM_STATE ** -0.5),
        "ssm_c_im": nrm(ks[20], (DEPTH, 2, SSM_GROUPS, SSM_GROUP, SSM_STATE), SSM_STATE ** -0.5),
        "ssm_d": nrm(ks[21], (DEPTH, SSM_GROUPS, SSM_GROUP), 1.0),
        "glu_w": nrm(ks[22], (DEPTH, B_WIDTH, B_WIDTH), B_WIDTH ** -0.5),
        "glu_b": nrm(ks[23], (DEPTH, B_WIDTH), 0.02),
        "pool_w": nrm(ks[24], (DEPTH, len(POOL_WINDOWS), POOL_GROUP, POOL_GROUP), POOL_GROUP ** -0.5),
        "pool_scale": 1.0 + nrm(ks[25], (DEPTH, C_WIDTH), 0.1),
        "ffn_w_gate": nrm(ks[26], (DEPTH, D_MODEL, D_FF), D_MODEL ** -0.5),
        "ffn_w_up": nrm(ks[27], (DEPTH, D_MODEL, D_FF), D_MODEL ** -0.5),
        "ffn_w_down": nrm(ks[28], (DEPTH, D_FF, D_MODEL), D_FF ** -0.5),
    }


def reference(x, c, ctx, c_ctx, w_mod, b_mod, norm_mix_pre, norm_mix_post, norm_ffn_pre, norm_ffn_post,
              w_in, w_out, sgu_w, sgu_b, ssm_lam_re, ssm_lam_im, ssm_log_dt, ssm_b_re, ssm_b_im,
              ssm_c_re, ssm_c_im, ssm_d, glu_w, glu_b, pool_w, pool_scale,
              ffn_w_gate, ffn_w_up, ffn_w_down):
    n_lat = x.shape[1]
    ROWS = n_lat // GRID_W
    x_lat = x + sincos_2d(ROWS, GRID_W, x.shape[-1]).astype(x.dtype)[None]
    x_ctx = ctx
    for i in range(DEPTH):
        need_ctx = i < DEPTH - 1
        mod_lat = jax.nn.silu(c) @ w_mod[i] + b_mod[i]
        mod_ctx = jax.nn.silu(c_ctx) @ w_mod[i] + b_mod[i]
        sh1, sc1, g1, sh2, sc2, g2 = [m[:, None, :] for m in jnp.split(mod_lat, 6, axis=-1)]
        csh1, csc1, cg1, csh2, csc2, cg2 = jnp.split(mod_ctx, 6, axis=-1)

        h_lat = modulate(rms_norm(x_lat, norm_mix_pre[i]), sh1, sc1)
        h_ctx = modulate(rms_norm(x_ctx, norm_mix_pre[i]), csh1, csc1)
        m_lat, m_ctx = mixing_sublayer(h_lat, h_ctx, ROWS, need_ctx, w_in[i], w_out[i], sgu_w[i], sgu_b[i],
                                       ssm_lam_re[i], ssm_lam_im[i], ssm_log_dt[i], ssm_b_re[i], ssm_b_im[i],
                                       ssm_c_re[i], ssm_c_im[i], ssm_d[i], glu_w[i], glu_b[i],
                                       pool_w[i], pool_scale[i])
        x_lat = x_lat + g1 * rms_norm(m_lat, norm_mix_post[i])
        f_lat = swiglu(modulate(rms_norm(x_lat, norm_ffn_pre[i]), sh2, sc2),
                       ffn_w_gate[i], ffn_w_up[i], ffn_w_down[i])
        x_lat = x_lat + g2 * rms_norm(f_lat, norm_ffn_post[i])
        if need_ctx:
            x_ctx = x_ctx + cg1 * rms_norm(m_ctx, norm_mix_post[i])
            f_ctx = swiglu(modulate(rms_norm(x_ctx, norm_ffn_pre[i]), csh2, csc2),
                           ffn_w_gate[i], ffn_w_up[i], ffn_w_down[i])
            x_ctx = x_ctx + cg2 * rms_norm(f_ctx, norm_ffn_post[i])
    return x_lat
```

```python
import functools
import math

import numpy as np
import jax
import jax.numpy as jnp
from jax import lax
from jax.experimental import pallas as pl
from jax.experimental.pallas import tpu as pltpu

D_MODEL = 1024
GRID_W = 64
EPS = 1e-6
A_WIDTH = 256
A_HEADS = 4
A_HEAD_DIM = 64
CHUNK = 128
B_WIDTH = 512
SSM_GROUP = 16
SSM_GROUPS = 32
SSM_STATE = 64
C_WIDTH = 256
POOL_WINDOWS = (2, 4, 8, 16)
POOL_GROUP = 64
D_IN = 1280
D_FF = 2816
S_LO = 2 * A_WIDTH
P_LO = S_LO + B_WIDTH
POOL_SEG = 256
FF_CHUNK = 256
N_FF_CHUNKS = D_FF // FF_CHUNK
MOD_ROWS = 40
V7X_VMEM_LIMIT = 56 * 1024 * 1024

F32 = jnp.float32
BF16 = jnp.bfloat16


def _rms(x, w):
    return x * lax.rsqrt(jnp.mean(x * x, axis=-1, keepdims=True) + EPS) * w


def _dot(a, b):
    return jnp.dot(a, b, preferred_element_type=F32)


def _dot_nt(a, b):
    return lax.dot_general(a, b, (((1,), (1,)), ((), ())), preferred_element_type=F32)


def _dot_tn(a, b):
    return lax.dot_general(a, b, (((0,), (0,)), ((), ())), preferred_element_type=F32)


def _mod_kernel(c_ref, w_ref, b_ref, o_ref):
    s = jax.nn.silu(c_ref[...])
    o_ref[0] = jnp.dot(s, w_ref[0], preferred_element_type=F32,
                       precision=lax.Precision.HIGHEST) + b_ref[0]


def _modulation(c_all, w_mod, b_mod):
    depth = w_mod.shape[0]
    n_tiles = 6
    return pl.pallas_call(
        _mod_kernel,
        grid=(depth, n_tiles),
        in_specs=[
            pl.BlockSpec((MOD_ROWS, D_MODEL), lambda i, j: (0, 0)),
            pl.BlockSpec((1, D_MODEL, D_MODEL), lambda i, j: (i, 0, j)),
            pl.BlockSpec((1, 1, D_MODEL), lambda i, j: (i, 0, j)),
        ],
        out_specs=pl.BlockSpec((1, MOD_ROWS, D_MODEL), lambda i, j: (i, 0, j)),
        out_shape=jax.ShapeDtypeStruct((depth, MOD_ROWS, 6 * D_MODEL), F32),
        name="modulation",
    )(c_all, w_mod, b_mod.reshape(depth, 1, 6 * D_MODEL))


def _proj_kernel(*refs, n_chunks, add_pe, ssm_only):
    it = iter(refs)
    x_ref = next(it)
    pe_ref = next(it) if add_pe else None
    mod_ref, nw_ref, win_ref = next(it), next(it), next(it)
    if not ssm_only:
        sguw_ref, sgub_ref, band_ref, icnt_ref, poolw_ref, pscale_ref = (next(it) for _ in range(6))
    u_ref = next(it)
    cat_ref = None if ssm_only else next(it)

    tt = n_chunks * CHUNK
    x = x_ref[0]
    if add_pe:
        x = x + pe_ref[...]
    h = _rms(x, nw_ref[...]) * (1.0 + mod_ref[1, 0]) + mod_ref[0, 0]
    hb = h.astype(BF16)

    zs = _dot_nt(win_ref[S_LO:P_LO, :], hb)
    for r in range(n_chunks):
        u_ref[:, r, 0, :, :] = zs[:, r * CHUNK:(r + 1) * CHUNK].reshape(SSM_GROUPS, SSM_GROUP, CHUNK)
    if ssm_only:
        return

    za = jax.nn.gelu(_dot_nt(win_ref[0:S_LO, :], hb))
    v = za[A_WIDTH:, :].reshape(A_HEADS, A_HEAD_DIM, tt)
    mu = jnp.mean(v, axis=1, keepdims=True)
    vc = v - mu
    var = jnp.mean(vc * vc, axis=1, keepdims=True)
    vn = (vc * lax.rsqrt(var + EPS)).astype(BF16)
    for hd in range(A_HEADS):
        lhs = jnp.concatenate([vn[hd][:, r * CHUNK:(r + 1) * CHUNK] for r in range(n_chunks)], axis=0)
        s = _dot(lhs, sguw_ref[hd]) + sgub_ref[hd]
        for r in range(n_chunks):
            ug = za[hd * A_HEAD_DIM:(hd + 1) * A_HEAD_DIM, r * CHUNK:(r + 1) * CHUNK]
            a = ug * s[r * A_HEAD_DIM:(r + 1) * A_HEAD_DIM, :]
            cat_ref[0, r, hd * A_HEAD_DIM:(hd + 1) * A_HEAD_DIM, :] = a.astype(BF16)

    zp = _dot_nt(win_ref[P_LO:D_IN, :], hb)
    n_seg = tt // POOL_SEG
    diffs = []
    for i in range(len(POOL_WINDOWS)):
        pg = zp[i * POOL_GROUP:(i + 1) * POOL_GROUP, :]
        lhs = jnp.concatenate([pg[:, j * POOL_SEG:(j + 1) * POOL_SEG] for j in range(n_seg)], axis=0)
        m = _dot(lhs.astype(BF16), band_ref[i]) * icnt_ref[i]
        m = jnp.concatenate([m[j * POOL_GROUP:(j + 1) * POOL_GROUP, :] for j in range(n_seg)], axis=1)
        diffs.append(m - pg)
    dt = jnp.concatenate(diffs, axis=0).astype(BF16)
    pool = _dot(poolw_ref[...], dt) * pscale_ref[...]
    for r in range(n_chunks):
        cat_ref[0, r, A_WIDTH:, :] = pool[:, r * CHUNK:(r + 1) * CHUNK].astype(BF16)


def _proj_call(x, pe, mod, mod_row, nw, win_t, local_w, *, n_chunks, ssm_only):
    bsz, seq, _ = x.shape
    tt = n_chunks * CHUNK
    n_tiles = seq // tt
    chunks = seq // CHUNK
    add_pe = pe is not None
    row = (lambda b: b) if mod_row is None else (lambda b: mod_row)
    const2 = lambda b, c: (0, 0)
    const3 = lambda b, c: (0, 0, 0)

    in_specs = [pl.BlockSpec((1, tt, D_MODEL), lambda b, c: (b, c, 0))]
    args = [x]
    if add_pe:
        in_specs.append(pl.BlockSpec((tt, D_MODEL), lambda b, c: (c, 0)))
        args.append(pe)
    in_specs += [
        pl.BlockSpec((6, 1, 1, D_MODEL), lambda b, c: (0, row(b), 0, 0)),
        pl.BlockSpec((1, D_MODEL), const2),
        pl.BlockSpec((D_IN, D_MODEL), const2),
    ]
    args += [mod, nw, win_t]
    if not ssm_only:
        sguw_t, sgub, band, icnt, poolw_t, pscale = local_w
        in_specs += [
            pl.BlockSpec((A_HEADS, CHUNK, CHUNK), const3),
            pl.BlockSpec((A_HEADS, 1, CHUNK), const3),
            pl.BlockSpec((len(POOL_WINDOWS), POOL_SEG, POOL_SEG), const3),
            pl.BlockSpec((len(POOL_WINDOWS), 1, POOL_SEG), const3),
            pl.BlockSpec((C_WIDTH, C_WIDTH), const2),
            pl.BlockSpec((C_WIDTH, 1), const2),
        ]
        args += [sguw_t, sgub, band, icnt, poolw_t, pscale]

    u_shape = jax.ShapeDtypeStruct((SSM_GROUPS, chunks, bsz, SSM_GROUP, CHUNK), F32)
    u_spec = pl.BlockSpec((SSM_GROUPS, n_chunks, 1, SSM_GROUP, CHUNK), lambda b, c: (0, c, b, 0, 0))
    if ssm_only:
        out_shape, out_specs = u_shape, u_spec
    else:
        out_shape = (u_shape, jax.ShapeDtypeStruct((bsz, chunks, 2 * A_WIDTH, CHUNK), BF16))
        out_specs = (u_spec, pl.BlockSpec((1, n_chunks, 2 * A_WIDTH, CHUNK), lambda b, c: (b, c, 0, 0)))

    return pl.pallas_call(
        functools.partial(_proj_kernel, n_chunks=n_chunks, add_pe=add_pe, ssm_only=ssm_only),
        grid=(bsz, n_tiles),
        in_specs=in_specs,
        out_specs=out_specs,
        out_shape=out_shape,
        compiler_params=pltpu.CompilerParams(
            dimension_semantics=("parallel", "parallel"), vmem_limit_bytes=V7X_VMEM_LIMIT),
        name="proj_ssm_only" if ssm_only else "proj_local_mix",
    )(*args)


def _ssm_kernel(*refs, rows_lat, rows_ctx, bsz, ctx_out):
    if ctx_out:
        ul_ref, uc_ref, kmat_ref, e_ref, min_ref, lam_ref, yl_ref, yc_ref, lhs_ref, sin_ref = refs
    else:
        ul_ref, uc_ref, kmat_ref, e_ref, min_ref, lam_ref, yl_ref, lhs_ref, sin_ref = refs
        yc_ref = None
    n = pl.program_id(1)
    n_lat = rows_lat // bsz
    n_ctx = rows_ctx // bsz
    half = 2 * SSM_STATE

    @pl.when(n == 0)
    def _():
        for hh in range(SSM_GROUP):
            lanes = slice(hh * CHUNK, (hh + 1) * CHUNK)
            lhs_ref[0:rows_lat, lanes] = ul_ref[0, pl.ds(hh, rows_lat, stride=SSM_GROUP), :].astype(BF16)
            lhs_ref[rows_lat:, lanes] = uc_ref[0, pl.ds(hh, rows_ctx, stride=SSM_GROUP), :].astype(BF16)
        sloc = _dot(lhs_ref[...], e_ref[0])
        lam_re = lam_ref[0, :, 0:half]
        lam_im = lam_ref[0, :, half:]
        fwd_lane = lax.broadcasted_iota(jnp.int32, (bsz, half), 1) < SSM_STATE

        def step(state, add):
            s_re, s_im = state
            a_re, a_im = add
            return (lam_re * s_re - lam_im * s_im + a_re, lam_re * s_im + lam_im * s_re + a_im)

        def rows_of(base, c):
            blk = sloc[base + c * bsz:base + (c + 1) * bsz, :]
            return blk[:, 0:half], blk[:, half:]

        def put(base, c, f_state, b_state):
            sin_ref[base + c * bsz:base + (c + 1) * bsz, 0:half] = jnp.where(
                fwd_lane, f_state[0], b_state[0]).astype(BF16)
            sin_ref[base + c * bsz:base + (c + 1) * bsz, half:] = jnp.where(
                fwd_lane, f_state[1], b_state[1]).astype(BF16)

        def sweep(base, count, f0, b0):
            f_in, b_in = [None] * count, [None] * count
            f, b = f0, b0
            for c in range(count):
                f_in[c] = f
                f = step(f, rows_of(base, c))
            f_end = f
            for c in reversed(range(count)):
                b_in[c] = b
                b = step(b, rows_of(base, c))
            for c in range(count):
                put(base, c, f_in[c], b_in[c])
            return f_end, b

        zero = (jnp.zeros((bsz, half), F32), jnp.zeros((bsz, half), F32))
        f_end, b_end = sweep(rows_lat, n_ctx, zero, zero)
        sweep(0, n_lat, f_end, b_end)

    n_cols = kmat_ref.shape[2]
    pair = 2 * CHUNK
    for q in range(n_cols // pair):
        cols = slice(q * pair, (q + 1) * pair)
        y = _dot(lhs_ref[...], kmat_ref[0, :, cols]) + _dot(sin_ref[...], min_ref[0, :, cols])
        for k in range(2):
            hh = n * (n_cols // CHUNK) + 2 * q + k
            piece = y[:, k * CHUNK:(k + 1) * CHUNK]
            yl_ref[0, pl.ds(hh, rows_lat, stride=SSM_GROUP), :] = piece[0:rows_lat]
            if ctx_out:
                yc_ref[0, pl.ds(hh, rows_ctx, stride=SSM_GROUP), :] = piece[rows_lat:]


def _ssm_call(u_lat, u_ctx, kmat, e_mat, m_in, lam_t, *, ctx_out):
    groups, c_lat, bsz = u_lat.shape[:3]
    c_ctx = u_ctx.shape[1]
    rows_lat, rows_ctx = c_lat * bsz, c_ctx * bsz
    rows = rows_lat + rows_ctx
    wide = SSM_GROUP * CHUNK
    n_split = 2
    ul = u_lat.reshape(groups, rows_lat * SSM_GROUP, CHUNK)
    uc = u_ctx.reshape(groups, rows_ctx * SSM_GROUP, CHUNK)
    per_g = lambda g, n: (g, 0, 0)
    out_shape = [jax.ShapeDtypeStruct(ul.shape, F32)]
    out_specs = [pl.BlockSpec((1, rows_lat * SSM_GROUP, CHUNK), per_g)]
    if ctx_out:
        out_shape.append(jax.ShapeDtypeStruct(uc.shape, F32))
        out_specs.append(pl.BlockSpec((1, rows_ctx * SSM_GROUP, CHUNK), per_g))
    outs = pl.pallas_call(
        functools.partial(_ssm_kernel, rows_lat=rows_lat, rows_ctx=rows_ctx, bsz=bsz, ctx_out=ctx_out),
        grid=(groups, n_split),
        in_specs=[
            pl.BlockSpec((1, rows_lat * SSM_GROUP, CHUNK), per_g),
            pl.BlockSpec((1, rows_ctx * SSM_GROUP, CHUNK), per_g),
            pl.BlockSpec((1, wide, wide // n_split), lambda g, n: (g, 0, n)),
            pl.BlockSpec((1, wide, 4 * SSM_STATE), per_g),
            pl.BlockSpec((1, 4 * SSM_STATE, wide // n_split), lambda g, n: (g, 0, n)),
            pl.BlockSpec((1, 1, 4 * SSM_STATE), per_g),
        ],
        out_specs=out_specs,
        out_shape=out_shape,
        scratch_shapes=[pltpu.VMEM((rows, wide), BF16), pltpu.VMEM((rows, 4 * SSM_STATE), BF16)],
        compiler_params=pltpu.CompilerParams(
            dimension_semantics=("parallel", "arbitrary"), vmem_limit_bytes=V7X_VMEM_LIMIT),
        name="ssm_chunked_scan",
    )(ul, uc, kmat, e_mat, m_in, lam_t)
    y_lat = outs[0].reshape(u_lat.shape)
    y_ctx = outs[1].reshape(u_ctx.shape) if ctx_out else None
    return y_lat, y_ctx


def _mixout_kernel(*refs, n_chunks, add_pe):
    it = iter(refs)
    x_ref = next(it)
    pe_ref = next(it) if add_pe else None
    y_ref, u_ref, cat_ref, mod_ref, nw_ref, d_ref, gluw_ref, glub_ref, wout_ref, o_ref = (next(it) for _ in range(10))

    def chunk(ref, r):
        return ref[:, r, 0, :, :].reshape(B_WIDTH, CHUNK)

    yt = jnp.concatenate([chunk(y_ref, r) + d_ref[...] * chunk(u_ref, r) for r in range(n_chunks)], axis=1)
    g = jax.nn.gelu(yt)
    gate = jax.nn.sigmoid(_dot(gluw_ref[...], g.astype(BF16)) + glub_ref[...])
    st = (g * gate).astype(BF16)
    at = jnp.concatenate([cat_ref[0, r, 0:A_WIDTH, :] for r in range(n_chunks)], axis=1)
    pt = jnp.concatenate([cat_ref[0, r, A_WIDTH:, :] for r in range(n_chunks)], axis=1)
    cat_t = jnp.concatenate([at, st, pt], axis=0)
    m = _dot_tn(cat_t, wout_ref[...])
    x = x_ref[0]
    if add_pe:
        x = x + pe_ref[...]
    o_ref[0] = x + mod_ref[2, 0] * _rms(m, nw_ref[...])


def _mixout_call(x, pe, y, u, cat, mod, mod_row, nw, d_col, gluw_t, glub_col, wout, *, n_chunks):
    bsz, seq, _ = x.shape
    tt = n_chunks * CHUNK
    add_pe = pe is not None
    row = (lambda b: b) if mod_row is None else (lambda b: mod_row)
    const2 = lambda b, c: (0, 0)
    in_specs = [pl.BlockSpec((1, tt, D_MODEL), lambda b, c: (b, c, 0))]
    args = [x]
    if add_pe:
        in_specs.append(pl.BlockSpec((tt, D_MODEL), lambda b, c: (c, 0)))
        args.append(pe)
    gspec = pl.BlockSpec((SSM_GROUPS, n_chunks, 1, SSM_GROUP, CHUNK), lambda b, c: (0, c, b, 0, 0))
    in_specs += [
        gspec, gspec,
        pl.BlockSpec((1, n_chunks, 2 * A_WIDTH, CHUNK), lambda b, c: (b, c, 0, 0)),
        pl.BlockSpec((6, 1, 1, D_MODEL), lambda b, c: (0, row(b), 0, 0)),
        pl.BlockSpec((1, D_MODEL), const2),
        pl.BlockSpec((B_WIDTH, 1), const2),
        pl.BlockSpec((B_WIDTH, B_WIDTH), const2),
        pl.BlockSpec((B_WIDTH, 1), const2),
        pl.BlockSpec((D_MODEL, D_MODEL), const2),
    ]
    args += [y, u, cat, mod, nw, d_col, gluw_t, glub_col, wout]
    return pl.pallas_call(
        functools.partial(_mixout_kernel, n_chunks=n_chunks, add_pe=add_pe),
        grid=(bsz, seq // tt),
        in_specs=in_specs,
        out_specs=pl.BlockSpec((1, tt, D_MODEL), lambda b, c: (b, c, 0)),
        out_shape=jax.ShapeDtypeStruct(x.shape, F32),
        compiler_params=pltpu.CompilerParams(
            dimension_semantics=("parallel", "parallel"), vmem_limit_bytes=V7X_VMEM_LIMIT),
        name="mix_out_residual",
    )(*args)


def _ffn_kernel(x_ref, mod_ref, nw_pre_ref, nw_post_ref, wg_ref, wu_ref, wd_ref, o_ref, acc_ref):
    x = x_ref[0]
    h = (_rms(x, nw_pre_ref[...]) * (1.0 + mod_ref[4, 0]) + mod_ref[3, 0]).astype(BF16)
    for k in range(N_FF_CHUNKS):
        act = (jax.nn.silu(_dot(h, wg_ref[k])) * _dot(h, wu_ref[k])).astype(BF16)
        part = _dot(act, wd_ref[k])
        if k == 0:
            acc_ref[...] = part
        else:
            acc_ref[...] += part
    o_ref[0] = x + mod_ref[5, 0] * _rms(acc_ref[...], nw_post_ref[...])


def _ffn_call(x, mod, mod_row, nw_pre, nw_post, wg, wu, wd, *, tile):
    bsz, seq, _ = x.shape
    row = (lambda b: b) if mod_row is None else (lambda b: mod_row)
    const2 = lambda b, c: (0, 0)
    const3 = lambda b, c: (0, 0, 0)
    return pl.pallas_call(
        _ffn_kernel,
        grid=(bsz, seq // tile),
        in_specs=[
            pl.BlockSpec((1, tile, D_MODEL), lambda b, c: (b, c, 0)),
            pl.BlockSpec((6, 1, 1, D_MODEL), lambda b, c: (0, row(b), 0, 0)),
            pl.BlockSpec((1, D_MODEL), const2),
            pl.BlockSpec((1, D_MODEL), const2),
            pl.BlockSpec((N_FF_CHUNKS, D_MODEL, FF_CHUNK), const3),
            pl.BlockSpec((N_FF_CHUNKS, D_MODEL, FF_CHUNK), const3),
            pl.BlockSpec((N_FF_CHUNKS, FF_CHUNK, D_MODEL), const3),
        ],
        out_specs=pl.BlockSpec((1, tile, D_MODEL), lambda b, c: (b, c, 0)),
        out_shape=jax.ShapeDtypeStruct(x.shape, F32),
        scratch_shapes=[pltpu.VMEM((tile, D_MODEL), F32)],
        compiler_params=pltpu.CompilerParams(
            dimension_semantics=("parallel", "parallel"), vmem_limit_bytes=V7X_VMEM_LIMIT),
        name="swiglu_ffn_residual",
    )(x, mod, nw_pre, nw_post, wg, wu, wd)


def _sincos_2d(rows, cols, dim):
    quarter = dim // 4
    omega = 1.0 / (10000.0 ** (jnp.arange(quarter, dtype=F32) / quarter))
    r = jnp.arange(rows, dtype=F32)[:, None] * omega
    cc = jnp.arange(cols, dtype=F32)[:, None] * omega
    er = jnp.concatenate([jnp.sin(r), jnp.cos(r)], axis=-1)
    ec = jnp.concatenate([jnp.sin(cc), jnp.cos(cc)], axis=-1)
    pe = jnp.concatenate([jnp.broadcast_to(er[:, None, :], (rows, cols, dim // 2)),
                          jnp.broadcast_to(ec[None, :, :], (rows, cols, dim // 2))], axis=-1)
    return pe.reshape(rows * cols, dim)


def _pool_bands(row_len):
    pos = np.arange(POOL_SEG)
    start = (pos // row_len) * row_len
    t = pos - start
    bands, icnts = [], []
    for w in POOL_WINDOWS:
        lo = np.clip(t - w // 2, 0, row_len) + start
        hi = np.clip(t - w // 2 + w, 0, row_len) + start
        bands.append(((pos[:, None] >= lo[None, :]) & (pos[:, None] < hi[None, :])).astype(np.float32))
        icnts.append((1.0 / (hi - lo).astype(np.float32))[None, :])
    return jnp.asarray(np.stack(bands), BF16), jnp.asarray(np.stack(icnts), F32)


def _ssm_operators(lam_re, lam_im, log_dt, b_re, b_im, c_re, c_im):
    t_len = CHUNK
    dt = jnp.exp(log_dt)[..., None]
    a_re, a_im = lam_re * dt, lam_im * dt
    mag = jnp.exp(a_re)
    lb_re, lb_im = mag * jnp.cos(a_im), mag * jnp.sin(a_im)
    den = lam_re * lam_re + lam_im * lam_im
    q_re = ((lb_re - 1.0) * lam_re + lb_im * lam_im) / den
    q_im = (lb_im * lam_re - (lb_re - 1.0) * lam_im) / den
    bb_re = q_re[..., None] * b_re - q_im[..., None] * b_im
    bb_im = q_re[..., None] * b_im + q_im[..., None] * b_re
    k = jnp.arange(t_len + 1, dtype=F32)
    p_mag = jnp.exp(a_re[..., None] * k)
    p_re = p_mag * jnp.cos(a_im[..., None] * k)
    p_im = p_mag * jnp.sin(a_im[..., None] * k)

    w_re = c_re[..., None] * p_re[:, :, None] - c_im[..., None] * p_im[:, :, None]
    w_im = c_re[..., None] * p_im[:, :, None] + c_im[..., None] * p_re[:, :, None]
    hp = lax.Precision.HIGHEST
    resp = (jnp.einsum('dgopk,dgph->dgkoh', w_re[..., :t_len], bb_re, precision=hp)
            - jnp.einsum('dgopk,dgph->dgkoh', w_im[..., :t_len], bb_im, precision=hp))
    s_idx = jnp.arange(t_len)[:, None]
    t_idx = jnp.arange(t_len)[None, :]
    lag_f = jnp.clip(t_idx - s_idx, 0, t_len - 1)
    lag_b = jnp.clip(s_idx - t_idx, 0, t_len - 1)
    kf = jnp.where((s_idx <= t_idx)[None, :, :, None, None], resp[0][:, lag_f], 0.0)
    kb = jnp.where((s_idx >= t_idx)[None, :, :, None, None], resp[1][:, lag_b], 0.0)
    groups = lam_re.shape[1]
    wide = SSM_GROUP * t_len
    kmat = jnp.transpose(kf + kb, (0, 4, 1, 3, 2)).reshape(groups, wide, wide)

    def end_map(pw_re, pw_im, d):
        re = bb_re[d][:, :, :, None] * pw_re[:, :, None, :] - bb_im[d][:, :, :, None] * pw_im[:, :, None, :]
        im = bb_re[d][:, :, :, None] * pw_im[:, :, None, :] + bb_im[d][:, :, :, None] * pw_re[:, :, None, :]
        return jnp.transpose(re, (0, 2, 3, 1)), jnp.transpose(im, (0, 2, 3, 1))

    ef_re, ef_im = end_map(p_re[0][..., t_len - 1::-1][..., :t_len], p_im[0][..., t_len - 1::-1][..., :t_len], 0)
    eb_re, eb_im = end_map(p_re[1][..., :t_len], p_im[1][..., :t_len], 1)
    e_mat = jnp.concatenate([ef_re, eb_re, ef_im, eb_im], axis=-1).reshape(groups, wide, 4 * SSM_STATE)

    def in_map(wr, wi):
        return jnp.transpose(wr, (0, 2, 1, 3)), -jnp.transpose(wi, (0, 2, 1, 3))

    mf_re, mf_im = in_map(w_re[0][..., 1:], w_im[0][..., 1:])
    mb_re, mb_im = in_map(w_re[1][..., :0:-1], w_im[1][..., :0:-1])
    m_in = jnp.concatenate([mf_re, mb_re, mf_im, mb_im], axis=1).reshape(groups, 4 * SSM_STATE, wide)

    lam_t = jnp.concatenate([p_re[0][..., t_len], p_re[1][..., t_len],
                             p_im[0][..., t_len], p_im[1][..., t_len]], axis=-1)[:, None, :]
    return kmat.astype(BF16), e_mat.astype(BF16), m_in.astype(BF16), lam_t


def _block_diag(blocks):
    n = len(blocks)
    rows = []
    for i, blk in enumerate(blocks):
        rows.append(jnp.concatenate([blk if j == i else jnp.zeros_like(blk) for j in range(n)], axis=1))
    return jnp.concatenate(rows, axis=0)


def kernel(x, c, ctx, c_ctx, w_mod, b_mod, norm_mix_pre, norm_mix_post, norm_ffn_pre, norm_ffn_post, w_in, w_out, sgu_w, sgu_b, ssm_lam_re, ssm_lam_im, ssm_log_dt, ssm_b_re, ssm_b_im, ssm_c_re, ssm_c_im, ssm_d, glu_w, glu_b, pool_w, pool_scale, ffn_w_gate, ffn_w_up, ffn_w_down):
    bsz, n_lat, _ = x.shape
    n_ctx = ctx.shape[1]
    depth = w_mod.shape[0]
    assert bsz + 1 <= MOD_ROWS and n_lat % (8 * CHUNK) == 0 and n_ctx == POOL_SEG

    pe = _sincos_2d(n_lat // GRID_W, GRID_W, D_MODEL)
    c_all = jnp.concatenate([c, c_ctx[None, :], jnp.zeros((MOD_ROWS - bsz - 1, D_MODEL), F32)], axis=0)
    mod_all = _modulation(c_all, w_mod, b_mod)
    mod_all = jnp.transpose(mod_all.reshape(depth, MOD_ROWS, 6, 1, D_MODEL), (0, 2, 1, 3, 4))
    ctx_row = bsz

    band_lat, icnt_lat = _pool_bands(GRID_W)
    band_ctx, icnt_ctx = _pool_bands(n_ctx)

    x_lat, x_ctx = x, ctx
    for i in range(depth):
        need_ctx = i < depth - 1
        first = i == 0
        mod = mod_all[i]
        win_t = w_in[i].T.astype(BF16)
        local_common = (jnp.transpose(sgu_w[i], (0, 2, 1)).astype(BF16), sgu_b[i][:, None, :])
        poolw_t = _block_diag([pool_w[i][k].T for k in range(len(POOL_WINDOWS))]).astype(BF16)
        pscale = pool_scale[i][:, None]
        local_lat = local_common + (band_lat, icnt_lat, poolw_t, pscale)
        local_ctx = local_common + (band_ctx, icnt_ctx, poolw_t, pscale)
        kmat, e_mat, m_in, lam_t = _ssm_operators(ssm_lam_re[i], ssm_lam_im[i], ssm_log_dt[i], ssm_b_re[i],
                                                  ssm_b_im[i], ssm_c_re[i], ssm_c_im[i])
        nw_mix_pre, nw_mix_post = norm_mix_pre[i][None, :], norm_mix_post[i][None, :]
        nw_ffn_pre, nw_ffn_post = norm_ffn_pre[i][None, :], norm_ffn_post[i][None, :]
        d_col = ssm_d[i].reshape(B_WIDTH, 1)
        gluw_t = glu_w[i].T.astype(BF16)
        glub_col = glu_b[i][:, None]
        wout = w_out[i].astype(BF16)
        wg = jnp.transpose(ffn_w_gate[i].reshape(D_MODEL, N_FF_CHUNKS, FF_CHUNK), (1, 0, 2)).astype(BF16)
        wu = jnp.transpose(ffn_w_up[i].reshape(D_MODEL, N_FF_CHUNKS, FF_CHUNK), (1, 0, 2)).astype(BF16)
        wd = ffn_w_down[i].reshape(N_FF_CHUNKS, FF_CHUNK, D_MODEL).astype(BF16)
        pe_i = pe if first else None

        u_lat, cat_lat = _proj_call(x_lat, pe_i, mod, None, nw_mix_pre, win_t, local_lat,
                                    n_chunks=8, ssm_only=False)
        if need_ctx:
            u_ctx, cat_ctx = _proj_call(x_ctx, None, mod, ctx_row, nw_mix_pre, win_t, local_ctx,
                                        n_chunks=2, ssm_only=False)
        else:
            u_ctx = _proj_call(x_ctx, None, mod, ctx_row, nw_mix_pre, win_t, None, n_chunks=2, ssm_only=True)
        y_lat, y_ctx = _ssm_call(u_lat, u_ctx, kmat, e_mat, m_in, lam_t, ctx_out=need_ctx)

        x_lat = _mixout_call(x_lat, pe_i, y_lat, u_lat, cat_lat, mod, None, nw_mix_post, d_col, gluw_t,
                             glub_col, wout, n_chunks=4)
        x_lat = _ffn_call(x_lat, mod, None, nw_ffn_pre, nw_ffn_post, wg, wu, wd, tile=512)
        if need_ctx:
            x_ctx = _mixout_call(x_ctx, None, y_ctx, u_ctx, cat_ctx, mod, ctx_row, nw_mix_post, d_col, gluw_t,
                                 glub_col, wout, n_chunks=2)
            x_ctx = _ffn_call(x_ctx, mod, ctx_row, nw_ffn_pre, nw_ffn_post, wg, wu, wd, tile=256)
    return x_lat
```

```python
import functools
import math

import numpy as np
import jax
import jax.numpy as jnp
from jax import lax
from jax.experimental import pallas as pl
from jax.experimental.pallas import tpu as pltpu

D_MODEL = 1024
GRID_W = 64
EPS = 1e-6
A_WIDTH = 256
A_HEADS = 4
A_HEAD_DIM = 64
CHUNK = 128
B_WIDTH = 512
SSM_GROUP = 16
SSM_GROUPS = 32
SSM_STATE = 64
C_WIDTH = 256
POOL_WINDOWS = (2, 4, 8, 16)
POOL_GROUP = 64
D_IN = 1280
D_FF = 2816
S_LO = 2 * A_WIDTH
P_LO = S_LO + B_WIDTH
POOL_SEG = 256
FF_CHUNK = 256
N_FF_CHUNKS = D_FF // FF_CHUNK
MOD_ROWS = 40
V7X_VMEM_LIMIT = 56 * 1024 * 1024

F32 = jnp.float32
BF16 = jnp.bfloat16


def _rms(x, w):
    return x * lax.rsqrt(jnp.mean(x * x, axis=-1, keepdims=True) + EPS) * w


def _dot(a, b):
    return jnp.dot(a, b, preferred_element_type=F32)


def _dot_nt(a, b):
    return lax.dot_general(a, b, (((1,), (1,)), ((), ())), preferred_element_type=F32)


def _dot_tn(a, b):
    return lax.dot_general(a, b, (((0,), (0,)), ((), ())), preferred_element_type=F32)


def _mod_kernel(c_ref, w_ref, b_ref, o_ref):
    s = jax.nn.silu(c_ref[...])
    o_ref[0] = jnp.dot(s, w_ref[0], preferred_element_type=F32,
                       precision=lax.Precision.HIGHEST) + b_ref[0]


def _modulation(c_all, w_mod, b_mod):
    depth = w_mod.shape[0]
    n_tiles = 6
    return pl.pallas_call(
        _mod_kernel,
        grid=(depth, n_tiles),
        in_specs=[
            pl.BlockSpec((MOD_ROWS, D_MODEL), lambda i, j: (0, 0)),
            pl.BlockSpec((1, D_MODEL, D_MODEL), lambda i, j: (i, 0, j)),
            pl.BlockSpec((1, 1, D_MODEL), lambda i, j: (i, 0, j)),
        ],
        out_specs=pl.BlockSpec((1, MOD_ROWS, D_MODEL), lambda i, j: (i, 0, j)),
        out_shape=jax.ShapeDtypeStruct((depth, MOD_ROWS, 6 * D_MODEL), F32),
        name="modulation",
    )(c_all, w_mod, b_mod.reshape(depth, 1, 6 * D_MODEL))


def _proj_kernel(*refs, n_chunks, add_pe, ssm_only):
    it = iter(refs)
    x_ref = next(it)
    pe_ref = next(it) if add_pe else None
    mod_ref, nw_ref, win_ref = next(it), next(it), next(it)
    if not ssm_only:
        sguw_ref, sgub_ref, band_ref, icnt_ref, poolw_ref, pscale_ref = (next(it) for _ in range(6))
    u_ref = next(it)
    cat_ref = None if ssm_only else next(it)

    tt = n_chunks * CHUNK
    x = x_ref[0]
    if add_pe:
        x = x + pe_ref[...]
    h = _rms(x, nw_ref[...]) * (1.0 + mod_ref[1, 0]) + mod_ref[0, 0]
    hb = h.astype(BF16)

    zs = _dot_nt(win_ref[S_LO:P_LO, :], hb)
    for r in range(n_chunks):
        u_ref[:, r, :, :] = zs[:, r * CHUNK:(r + 1) * CHUNK].reshape(SSM_GROUPS, SSM_GROUP, CHUNK)
    if ssm_only:
        return

    za = jax.nn.gelu(_dot_nt(win_ref[0:S_LO, :], hb))
    v = za[A_WIDTH:, :].reshape(A_HEADS, A_HEAD_DIM, tt)
    mu = jnp.mean(v, axis=1, keepdims=True)
    vc = v - mu
    var = jnp.mean(vc * vc, axis=1, keepdims=True)
    vn = (vc * lax.rsqrt(var + EPS)).astype(BF16)
    for hd in range(A_HEADS):
        lhs = jnp.concatenate([vn[hd][:, r * CHUNK:(r + 1) * CHUNK] for r in range(n_chunks)], axis=0)
        s = _dot(lhs, sguw_ref[hd]) + sgub_ref[hd]
        for r in range(n_chunks):
            ug = za[hd * A_HEAD_DIM:(hd + 1) * A_HEAD_DIM, r * CHUNK:(r + 1) * CHUNK]
            a = ug * s[r * A_HEAD_DIM:(r + 1) * A_HEAD_DIM, :]
            cat_ref[0, r, hd * A_HEAD_DIM:(hd + 1) * A_HEAD_DIM, :] = a.astype(BF16)

    zp = _dot_nt(win_ref[P_LO:D_IN, :], hb)
    n_seg = tt // POOL_SEG
    diffs = []
    for i in range(len(POOL_WINDOWS)):
        pg = zp[i * POOL_GROUP:(i + 1) * POOL_GROUP, :]
        lhs = jnp.concatenate([pg[:, j * POOL_SEG:(j + 1) * POOL_SEG] for j in range(n_seg)], axis=0)
        m = _dot(lhs.astype(BF16), band_ref[i]) * icnt_ref[i]
        m = jnp.concatenate([m[j * POOL_GROUP:(j + 1) * POOL_GROUP, :] for j in range(n_seg)], axis=1)
        diffs.append(m - pg)
    dt = jnp.concatenate(diffs, axis=0).astype(BF16)
    pool = _dot(poolw_ref[...], dt) * pscale_ref[...]
    for r in range(n_chunks):
        cat_ref[0, r, A_WIDTH:, :] = pool[:, r * CHUNK:(r + 1) * CHUNK].astype(BF16)


def _proj_call(x, pe, mod, mod_row, nw, win_t, local_w, *, n_chunks, ssm_only):
    bsz, seq, _ = x.shape
    tt = n_chunks * CHUNK
    n_tiles = seq // tt
    chunks = seq // CHUNK
    add_pe = pe is not None
    row = (lambda b: b) if mod_row is None else (lambda b: mod_row)
    const2 = lambda b, c: (0, 0)
    const3 = lambda b, c: (0, 0, 0)

    in_specs = [pl.BlockSpec((1, tt, D_MODEL), lambda b, c: (b, c, 0))]
    args = [x]
    if add_pe:
        in_specs.append(pl.BlockSpec((tt, D_MODEL), lambda b, c: (c, 0)))
        args.append(pe)
    in_specs += [
        pl.BlockSpec((6, 1, 1, D_MODEL), lambda b, c: (0, row(b), 0, 0)),
        pl.BlockSpec((1, D_MODEL), const2),
        pl.BlockSpec((D_IN, D_MODEL), const2),
    ]
    args += [mod, nw, win_t]
    if not ssm_only:
        sguw_t, sgub, band, icnt, poolw_t, pscale = local_w
        in_specs += [
            pl.BlockSpec((A_HEADS, CHUNK, CHUNK), const3),
            pl.BlockSpec((A_HEADS, 1, CHUNK), const3),
            pl.BlockSpec((len(POOL_WINDOWS), POOL_SEG, POOL_SEG), const3),
            pl.BlockSpec((len(POOL_WINDOWS), 1, POOL_SEG), const3),
            pl.BlockSpec((C_WIDTH, C_WIDTH), const2),
            pl.BlockSpec((C_WIDTH, 1), const2),
        ]
        args += [sguw_t, sgub, band, icnt, poolw_t, pscale]

    u_shape = jax.ShapeDtypeStruct((SSM_GROUPS, chunks, bsz * SSM_GROUP, CHUNK), F32)
    u_spec = pl.BlockSpec((SSM_GROUPS, n_chunks, SSM_GROUP, CHUNK), lambda b, c: (0, c, b, 0))
    if ssm_only:
        out_shape, out_specs = u_shape, u_spec
    else:
        out_shape = (u_shape, jax.ShapeDtypeStruct((bsz, chunks, 2 * A_WIDTH, CHUNK), BF16))
        out_specs = (u_spec, pl.BlockSpec((1, n_chunks, 2 * A_WIDTH, CHUNK), lambda b, c: (b, c, 0, 0)))

    return pl.pallas_call(
        functools.partial(_proj_kernel, n_chunks=n_chunks, add_pe=add_pe, ssm_only=ssm_only),
        grid=(bsz, n_tiles),
        in_specs=in_specs,
        out_specs=out_specs,
        out_shape=out_shape,
        compiler_params=pltpu.CompilerParams(
            dimension_semantics=("parallel", "parallel"), vmem_limit_bytes=V7X_VMEM_LIMIT),
        name="proj_ssm_only" if ssm_only else "proj_local_mix",
    )(*args)


def _ssm_kernel(*refs, rows_lat, rows_ctx, bsz, ctx_out):
    if ctx_out:
        ul_ref, uc_ref, kmat_ref, e_ref, min_ref, lam_ref, yl_ref, yc_ref, lhs_ref, sin_ref = refs
    else:
        ul_ref, uc_ref, kmat_ref, e_ref, min_ref, lam_ref, yl_ref, lhs_ref, sin_ref = refs
        yc_ref = None
    n = pl.program_id(1)
    n_lat = rows_lat // bsz
    n_ctx = rows_ctx // bsz
    half = 2 * SSM_STATE

    @pl.when(n == 0)
    def _():
        for hh in range(SSM_GROUP):
            lanes = slice(hh * CHUNK, (hh + 1) * CHUNK)
            lhs_ref[0:rows_lat, lanes] = ul_ref[0, pl.ds(hh, rows_lat, stride=SSM_GROUP), :].astype(BF16)
            lhs_ref[rows_lat:, lanes] = uc_ref[0, pl.ds(hh, rows_ctx, stride=SSM_GROUP), :].astype(BF16)
        sloc = _dot(lhs_ref[...], e_ref[0])
        lam_re = lam_ref[0, :, 0:half]
        lam_im = lam_ref[0, :, half:]
        fwd_lane = lax.broadcasted_iota(jnp.int32, (bsz, half), 1) < SSM_STATE

        def step(state, add):
            s_re, s_im = state
            a_re, a_im = add
            return (lam_re * s_re - lam_im * s_im + a_re, lam_re * s_im + lam_im * s_re + a_im)

        def rows_of(base, c):
            blk = sloc[base + c * bsz:base + (c + 1) * bsz, :]
            return blk[:, 0:half], blk[:, half:]

        def put(base, c, f_state, b_state):
            sin_ref[base + c * bsz:base + (c + 1) * bsz, 0:half] = jnp.where(
                fwd_lane, f_state[0], b_state[0]).astype(BF16)
            sin_ref[base + c * bsz:base + (c + 1) * bsz, half:] = jnp.where(
                fwd_lane, f_state[1], b_state[1]).astype(BF16)

        def sweep(base, count, f0, b0):
            f_in, b_in = [None] * count, [None] * count
            f, b = f0, b0
            for c in range(count):
                f_in[c] = f
                f = step(f, rows_of(base, c))
            f_end = f
            for c in reversed(range(count)):
                b_in[c] = b
                b = step(b, rows_of(base, c))
            for c in range(count):
                put(base, c, f_in[c], b_in[c])
            return f_end, b

        zero = (jnp.zeros((bsz, half), F32), jnp.zeros((bsz, half), F32))
        f_end, b_end = sweep(rows_lat, n_ctx, zero, zero)
        sweep(0, n_lat, f_end, b_end)

    n_cols = kmat_ref.shape[2]
    pair = 2 * CHUNK
    for q in range(n_cols // pair):
        cols = slice(q * pair, (q + 1) * pair)
        y = _dot(lhs_ref[...], kmat_ref[0, :, cols]) + _dot(sin_ref[...], min_ref[0, :, cols])
        for k in range(2):
            hh = n * (n_cols // CHUNK) + 2 * q + k
            piece = y[:, k * CHUNK:(k + 1) * CHUNK]
            yl_ref[0, pl.ds(hh, rows_lat, stride=SSM_GROUP), :] = piece[0:rows_lat]
            if ctx_out:
                yc_ref[0, pl.ds(hh, rows_ctx, stride=SSM_GROUP), :] = piece[rows_lat:]


def _ssm_call(u_lat, u_ctx, kmat, e_mat, m_in, lam_t, *, ctx_out):
    groups, c_lat = u_lat.shape[:2]
    c_ctx = u_ctx.shape[1]
    bsz = u_lat.shape[2] // SSM_GROUP
    rows_lat, rows_ctx = c_lat * bsz, c_ctx * bsz
    rows = rows_lat + rows_ctx
    wide = SSM_GROUP * CHUNK
    n_split = 2
    ul = u_lat.reshape(groups, rows_lat * SSM_GROUP, CHUNK)
    uc = u_ctx.reshape(groups, rows_ctx * SSM_GROUP, CHUNK)
    per_g = lambda g, n: (g, 0, 0)
    out_shape = [jax.ShapeDtypeStruct(ul.shape, F32)]
    out_specs = [pl.BlockSpec((1, rows_lat * SSM_GROUP, CHUNK), per_g)]
    if ctx_out:
        out_shape.append(jax.ShapeDtypeStruct(uc.shape, F32))
        out_specs.append(pl.BlockSpec((1, rows_ctx * SSM_GROUP, CHUNK), per_g))
    outs = pl.pallas_call(
        functools.partial(_ssm_kernel, rows_lat=rows_lat, rows_ctx=rows_ctx, bsz=bsz, ctx_out=ctx_out),
        grid=(groups, n_split),
        in_specs=[
            pl.BlockSpec((1, rows_lat * SSM_GROUP, CHUNK), per_g),
            pl.BlockSpec((1, rows_ctx * SSM_GROUP, CHUNK), per_g),
            pl.BlockSpec((1, wide, wide // n_split), lambda g, n: (g, 0, n)),
            pl.BlockSpec((1, wide, 4 * SSM_STATE), per_g),
            pl.BlockSpec((1, 4 * SSM_STATE, wide // n_split), lambda g, n: (g, 0, n)),
            pl.BlockSpec((1, 1, 4 * SSM_STATE), per_g),
        ],
        out_specs=out_specs,
        out_shape=out_shape,
        scratch_shapes=[pltpu.VMEM((rows, wide), BF16), pltpu.VMEM((rows, 4 * SSM_STATE), BF16)],
        compiler_params=pltpu.CompilerParams(
            dimension_semantics=("parallel", "arbitrary"), vmem_limit_bytes=V7X_VMEM_LIMIT),
        name="ssm_chunked_scan",
    )(ul, uc, kmat, e_mat, m_in, lam_t)
    y_lat = outs[0].reshape(u_lat.shape)
    y_ctx = outs[1].reshape(u_ctx.shape) if ctx_out else None
    return y_lat, y_ctx


def _toeplitz_kernel(r_ref, o_ref):
    def per_in_channel(hh, carry):
        rows = pl.ds(pl.multiple_of(hh * CHUNK, CHUNK), CHUNK)
        for oo in range(SSM_GROUP):
            table = jnp.broadcast_to(r_ref[0, hh, oo:oo + 1, :], (CHUNK, 2 * CHUNK))
            shifted = pltpu.roll(table, 0, 1, stride=1, stride_axis=0)
            o_ref[0, rows, oo * CHUNK:(oo + 1) * CHUNK] = shifted[:, CHUNK:].astype(BF16)
        return carry

    lax.fori_loop(0, SSM_GROUP, per_in_channel, 0)


def _toeplitz_call(lag_table):
    groups = lag_table.shape[0]
    wide = SSM_GROUP * CHUNK
    return pl.pallas_call(
        _toeplitz_kernel,
        grid=(groups,),
        in_specs=[pl.BlockSpec((1, SSM_GROUP, SSM_GROUP, 2 * CHUNK), lambda g: (g, 0, 0, 0))],
        out_specs=pl.BlockSpec((1, wide, wide), lambda g: (g, 0, 0)),
        out_shape=jax.ShapeDtypeStruct((groups, wide, wide), BF16),
        compiler_params=pltpu.CompilerParams(
            dimension_semantics=("parallel",), vmem_limit_bytes=V7X_VMEM_LIMIT),
        name="ssm_toeplitz_build",
    )(lag_table)


def _mixout_kernel(*refs, n_chunks, add_pe):
    it = iter(refs)
    x_ref = next(it)
    pe_ref = next(it) if add_pe else None
    y_ref, u_ref, cat_ref, mod_ref, nw_ref, d_ref, gluw_ref, glub_ref, wout_ref, o_ref = (next(it) for _ in range(10))

    def chunk(ref, r):
        return ref[:, r, :, :].reshape(B_WIDTH, CHUNK)

    yt = jnp.concatenate([chunk(y_ref, r) + d_ref[...] * chunk(u_ref, r) for r in range(n_chunks)], axis=1)
    g = jax.nn.gelu(yt)
    gate = jax.nn.sigmoid(_dot(gluw_ref[...], g.astype(BF16)) + glub_ref[...])
    st = (g * gate).astype(BF16)
    at = jnp.concatenate([cat_ref[0, r, 0:A_WIDTH, :] for r in range(n_chunks)], axis=1)
    pt = jnp.concatenate([cat_ref[0, r, A_WIDTH:, :] for r in range(n_chunks)], axis=1)
    cat_t = jnp.concatenate([at, st, pt], axis=0)
    m = _dot_tn(cat_t, wout_ref[...])
    x = x_ref[0]
    if add_pe:
        x = x + pe_ref[...]
    o_ref[0] = x + mod_ref[2, 0] * _rms(m, nw_ref[...])


def _mixout_call(x, pe, y, u, cat, mod, mod_row, nw, d_col, gluw_t, glub_col, wout, *, n_chunks):
    bsz, seq, _ = x.shape
    tt = n_chunks * CHUNK
    add_pe = pe is not None
    row = (lambda b: b) if mod_row is None else (lambda b: mod_row)
    const2 = lambda b, c: (0, 0)
    in_specs = [pl.BlockSpec((1, tt, D_MODEL), lambda b, c: (b, c, 0))]
    args = [x]
    if add_pe:
        in_specs.append(pl.BlockSpec((tt, D_MODEL), lambda b, c: (c, 0)))
        args.append(pe)
    gspec = pl.BlockSpec((SSM_GROUPS, n_chunks, SSM_GROUP, CHUNK), lambda b, c: (0, c, b, 0))
    in_specs += [
        gspec, gspec,
        pl.BlockSpec((1, n_chunks, 2 * A_WIDTH, CHUNK), lambda b, c: (b, c, 0, 0)),
        pl.BlockSpec((6, 1, 1, D_MODEL), lambda b, c: (0, row(b), 0, 0)),
        pl.BlockSpec((1, D_MODEL), const2),
        pl.BlockSpec((B_WIDTH, 1), const2),
        pl.BlockSpec((B_WIDTH, B_WIDTH), const2),
        pl.BlockSpec((B_WIDTH, 1), const2),
        pl.BlockSpec((D_MODEL, D_MODEL), const2),
    ]
    args += [y, u, cat, mod, nw, d_col, gluw_t, glub_col, wout]
    return pl.pallas_call(
        functools.partial(_mixout_kernel, n_chunks=n_chunks, add_pe=add_pe),
        grid=(bsz, seq // tt),
        in_specs=in_specs,
        out_specs=pl.BlockSpec((1, tt, D_MODEL), lambda b, c: (b, c, 0)),
        out_shape=jax.ShapeDtypeStruct(x.shape, F32),
        compiler_params=pltpu.CompilerParams(
            dimension_semantics=("parallel", "parallel"), vmem_limit_bytes=V7X_VMEM_LIMIT),
        name="mix_out_residual",
    )(*args)


def _ffn_kernel(x_ref, mod_ref, nw_pre_ref, nw_post_ref, wg_ref, wu_ref, wd_ref, o_ref, acc_ref):
    x = x_ref[0]
    h = (_rms(x, nw_pre_ref[...]) * (1.0 + mod_ref[4, 0]) + mod_ref[3, 0]).astype(BF16)
    for k in range(N_FF_CHUNKS):
        act = (jax.nn.silu(_dot(h, wg_ref[k])) * _dot(h, wu_ref[k])).astype(BF16)
        part = _dot(act, wd_ref[k])
        if k == 0:
            acc_ref[...] = part
        else:
            acc_ref[...] += part
    o_ref[0] = x + mod_ref[5, 0] * _rms(acc_ref[...], nw_post_ref[...])


def _ffn_call(x, mod, mod_row, nw_pre, nw_post, wg, wu, wd, *, tile):
    bsz, seq, _ = x.shape
    row = (lambda b: b) if mod_row is None else (lambda b: mod_row)
    const2 = lambda b, c: (0, 0)
    const3 = lambda b, c: (0, 0, 0)
    return pl.pallas_call(
        _ffn_kernel,
        grid=(bsz, seq // tile),
        in_specs=[
            pl.BlockSpec((1, tile, D_MODEL), lambda b, c: (b, c, 0)),
            pl.BlockSpec((6, 1, 1, D_MODEL), lambda b, c: (0, row(b), 0, 0)),
            pl.BlockSpec((1, D_MODEL), const2),
            pl.BlockSpec((1, D_MODEL), const2),
            pl.BlockSpec((N_FF_CHUNKS, D_MODEL, FF_CHUNK), const3),
            pl.BlockSpec((N_FF_CHUNKS, D_MODEL, FF_CHUNK), const3),
            pl.BlockSpec((N_FF_CHUNKS, FF_CHUNK, D_MODEL), const3),
        ],
        out_specs=pl.BlockSpec((1, tile, D_MODEL), lambda b, c: (b, c, 0)),
        out_shape=jax.ShapeDtypeStruct(x.shape, F32),
        scratch_shapes=[pltpu.VMEM((tile, D_MODEL), F32)],
        compiler_params=pltpu.CompilerParams(
            dimension_semantics=("parallel", "parallel"), vmem_limit_bytes=V7X_VMEM_LIMIT),
        name="swiglu_ffn_residual",
    )(x, mod, nw_pre, nw_post, wg, wu, wd)


def _sincos_2d(rows, cols, dim):
    quarter = dim // 4
    omega = 1.0 / (10000.0 ** (jnp.arange(quarter, dtype=F32) / quarter))
    r = jnp.arange(rows, dtype=F32)[:, None] * omega
    cc = jnp.arange(cols, dtype=F32)[:, None] * omega
    er = jnp.concatenate([jnp.sin(r), jnp.cos(r)], axis=-1)
    ec = jnp.concatenate([jnp.sin(cc), jnp.cos(cc)], axis=-1)
    pe = jnp.concatenate([jnp.broadcast_to(er[:, None, :], (rows, cols, dim // 2)),
                          jnp.broadcast_to(ec[None, :, :], (rows, cols, dim // 2))], axis=-1)
    return pe.reshape(rows * cols, dim)


def _pool_bands(row_len):
    pos = np.arange(POOL_SEG)
    start = (pos // row_len) * row_len
    t = pos - start
    bands, icnts = [], []
    for w in POOL_WINDOWS:
        lo = np.clip(t - w // 2, 0, row_len) + start
        hi = np.clip(t - w // 2 + w, 0, row_len) + start
        bands.append(((pos[:, None] >= lo[None, :]) & (pos[:, None] < hi[None, :])).astype(np.float32))
        icnts.append((1.0 / (hi - lo).astype(np.float32))[None, :])
    return jnp.asarray(np.stack(bands), BF16), jnp.asarray(np.stack(icnts), F32)


def _ssm_operators(lam_re, lam_im, log_dt, b_re, b_im, c_re, c_im):
    t_len = CHUNK
    dt = jnp.exp(log_dt)[..., None]
    a_re, a_im = lam_re * dt, lam_im * dt
    mag = jnp.exp(a_re)
    lb_re, lb_im = mag * jnp.cos(a_im), mag * jnp.sin(a_im)
    den = lam_re * lam_re + lam_im * lam_im
    q_re = ((lb_re - 1.0) * lam_re + lb_im * lam_im) / den
    q_im = (lb_im * lam_re - (lb_re - 1.0) * lam_im) / den
    bb_re = q_re[..., None] * b_re - q_im[..., None] * b_im
    bb_im = q_re[..., None] * b_im + q_im[..., None] * b_re
    k = jnp.arange(t_len + 1, dtype=F32)
    p_mag = jnp.exp(a_re[..., None] * k)
    p_re = p_mag * jnp.cos(a_im[..., None] * k)
    p_im = p_mag * jnp.sin(a_im[..., None] * k)

    w_re = c_re[..., None] * p_re[:, :, None] - c_im[..., None] * p_im[:, :, None]
    w_im = c_re[..., None] * p_im[:, :, None] + c_im[..., None] * p_re[:, :, None]
    hp = lax.Precision.HIGHEST
    resp = (jnp.einsum('dgopk,dgph->dgkoh', w_re[..., :t_len], bb_re, precision=hp)
            - jnp.einsum('dgopk,dgph->dgkoh', w_im[..., :t_len], bb_im, precision=hp))
    r_f = jnp.transpose(resp[0], (0, 3, 2, 1))
    r_b = jnp.transpose(resp[1], (0, 3, 2, 1))
    lag_table = jnp.concatenate([jnp.zeros_like(r_f[..., :1]), r_b[..., :0:-1],
                                 r_f[..., :1] + r_b[..., :1], r_f[..., 1:]], axis=-1)
    groups = lam_re.shape[1]
    wide = SSM_GROUP * t_len
    kmat = _toeplitz_call(lag_table)

    def end_map(pw_re, pw_im, d):
        re = bb_re[d][:, :, :, None] * pw_re[:, :, None, :] - bb_im[d][:, :, :, None] * pw_im[:, :, None, :]
        im = bb_re[d][:, :, :, None] * pw_im[:, :, None, :] + bb_im[d][:, :, :, None] * pw_re[:, :, None, :]
        return jnp.transpose(re, (0, 2, 3, 1)), jnp.transpose(im, (0, 2, 3, 1))

    ef_re, ef_im = end_map(p_re[0][..., t_len - 1::-1][..., :t_len], p_im[0][..., t_len - 1::-1][..., :t_len], 0)
    eb_re, eb_im = end_map(p_re[1][..., :t_len], p_im[1][..., :t_len], 1)
    e_mat = jnp.concatenate([ef_re, eb_re, ef_im, eb_im], axis=-1).reshape(groups, wide, 4 * SSM_STATE)

    def in_map(wr, wi):
        return jnp.transpose(wr, (0, 2, 1, 3)), -jnp.transpose(wi, (0, 2, 1, 3))

    mf_re, mf_im = in_map(w_re[0][..., 1:], w_im[0][..., 1:])
    mb_re, mb_im = in_map(w_re[1][..., :0:-1], w_im[1][..., :0:-1])
    m_in = jnp.concatenate([mf_re, mb_re, mf_im, mb_im], axis=1).reshape(groups, 4 * SSM_STATE, wide)

    lam_t = jnp.concatenate([p_re[0][..., t_len], p_re[1][..., t_len],
                             p_im[0][..., t_len], p_im[1][..., t_len]], axis=-1)[:, None, :]
    return kmat, e_mat.astype(BF16), m_in.astype(BF16), lam_t


def _block_diag(blocks):
    n = len(blocks)
    rows = []
    for i, blk in enumerate(blocks):
        rows.append(jnp.concatenate([blk if j == i else jnp.zeros_like(blk) for j in range(n)], axis=1))
    return jnp.concatenate(rows, axis=0)


def kernel(x, c, ctx, c_ctx, w_mod, b_mod, norm_mix_pre, norm_mix_post, norm_ffn_pre, norm_ffn_post, w_in, w_out, sgu_w, sgu_b, ssm_lam_re, ssm_lam_im, ssm_log_dt, ssm_b_re, ssm_b_im, ssm_c_re, ssm_c_im, ssm_d, glu_w, glu_b, pool_w, pool_scale, ffn_w_gate, ffn_w_up, ffn_w_down):
    bsz, n_lat, _ = x.shape
    n_ctx = ctx.shape[1]
    depth = w_mod.shape[0]
    assert bsz + 1 <= MOD_ROWS and n_lat % (8 * CHUNK) == 0 and n_ctx == POOL_SEG

    pe = _sincos_2d(n_lat // GRID_W, GRID_W, D_MODEL)
    c_all = jnp.concatenate([c, c_ctx[None, :], jnp.zeros((MOD_ROWS - bsz - 1, D_MODEL), F32)], axis=0)
    mod_all = _modulation(c_all, w_mod, b_mod)
    mod_all = jnp.transpose(mod_all.reshape(depth, MOD_ROWS, 6, 1, D_MODEL), (0, 2, 1, 3, 4))
    ctx_row = bsz

    band_lat, icnt_lat = _pool_bands(GRID_W)
    band_ctx, icnt_ctx = _pool_bands(n_ctx)

    x_lat, x_ctx = x, ctx
    for i in range(depth):
        need_ctx = i < depth - 1
        first = i == 0
        mod = mod_all[i]
        win_t = w_in[i].T.astype(BF16)
        local_common = (jnp.transpose(sgu_w[i], (0, 2, 1)).astype(BF16), sgu_b[i][:, None, :])
        poolw_t = _block_diag([pool_w[i][k].T for k in range(len(POOL_WINDOWS))]).astype(BF16)
        pscale = pool_scale[i][:, None]
        local_lat = local_common + (band_lat, icnt_lat, poolw_t, pscale)
        local_ctx = local_common + (band_ctx, icnt_ctx, poolw_t, pscale)
        kmat, e_mat, m_in, lam_t = _ssm_operators(ssm_lam_re[i], ssm_lam_im[i], ssm_log_dt[i], ssm_b_re[i],
                                                  ssm_b_im[i], ssm_c_re[i], ssm_c_im[i])
        nw_mix_pre, nw_mix_post = norm_mix_pre[i][None, :], norm_mix_post[i][None, :]
        nw_ffn_pre, nw_ffn_post = norm_ffn_pre[i][None, :], norm_ffn_post[i][None, :]
        d_col = ssm_d[i].reshape(B_WIDTH, 1)
        gluw_t = glu_w[i].T.astype(BF16)
        glub_col = glu_b[i][:, None]
        wout = w_out[i].astype(BF16)
        wg = jnp.transpose(ffn_w_gate[i].reshape(D_MODEL, N_FF_CHUNKS, FF_CHUNK), (1, 0, 2)).astype(BF16)
        wu = jnp.transpose(ffn_w_up[i].reshape(D_MODEL, N_FF_CHUNKS, FF_CHUNK), (1, 0, 2)).astype(BF16)
        wd = ffn_w_down[i].reshape(N_FF_CHUNKS, FF_CHUNK, D_MODEL).astype(BF16)
        pe_i = pe if first else None

        u_lat, cat_lat = _proj_call(x_lat, pe_i, mod, None, nw_mix_pre, win_t, local_lat,
                                    n_chunks=8, ssm_only=False)
        if need_ctx:
            u_ctx, cat_ctx = _proj_call(x_ctx, None, mod, ctx_row, nw_mix_pre, win_t, local_ctx,
                                        n_chunks=2, ssm_only=False)
        else:
            u_ctx = _proj_call(x_ctx, None, mod, ctx_row, nw_mix_pre, win_t, None, n_chunks=2, ssm_only=True)
        y_lat, y_ctx = _ssm_call(u_lat, u_ctx, kmat, e_mat, m_in, lam_t, ctx_out=need_ctx)

        x_lat = _mixout_call(x_lat, pe_i, y_lat, u_lat, cat_lat, mod, None, nw_mix_post, d_col, gluw_t,
                             glub_col, wout, n_chunks=4)
        x_lat = _ffn_call(x_lat, mod, None, nw_ffn_pre, nw_ffn_post, wg, wu, wd, tile=512)
        if need_ctx:
            x_ctx = _mixout_call(x_ctx, None, y_ctx, u_ctx, cat_ctx, mod, ctx_row, nw_mix_post, d_col, gluw_t,
                                 glub_col, wout, n_chunks=2)
            x_ctx = _ffn_call(x_ctx, mod, ctx_row, nw_ffn_pre, nw_ffn_post, wg, wu, wd, tile=256)
    return x_lat
```

```python
import functools

import numpy as np
import jax
import jax.numpy as jnp
from jax import lax
from jax.experimental import pallas as pl
from jax.experimental.pallas import tpu as pltpu

D_MODEL = 1024
GRID_W = 64
EPS = 1e-6
A_WIDTH = 256
A_HEADS = 4
A_HEAD_DIM = 64
CHUNK = 128
B_WIDTH = 512
SSM_GROUP = 16
SSM_GROUPS = 32
SSM_STATE = 64
C_WIDTH = 256
POOL_WINDOWS = (2, 4, 8, 16)
POOL_GROUP = 64
D_IN = 1280
D_FF = 2816
S_LO = 2 * A_WIDTH
P_LO = S_LO + B_WIDTH
POOL_SEG = 256
FF_CHUNK = 256
N_FF_CHUNKS = D_FF // FF_CHUNK
SSM_WIDE = SSM_GROUP * CHUNK
MOD_ROWS = 40
V7X_VMEM_LIMIT = 56 * 1024 * 1024

F32 = jnp.float32
BF16 = jnp.bfloat16


def _rms(x, w):
    return x * lax.rsqrt(jnp.mean(x * x, axis=-1, keepdims=True) + EPS) * w


def _dot(a, b):
    return jnp.dot(a, b, preferred_element_type=F32)


def _dot_nt(a, b):
    return lax.dot_general(a, b, (((1,), (1,)), ((), ())), preferred_element_type=F32)


def _dot_tn(a, b):
    return lax.dot_general(a, b, (((0,), (0,)), ((), ())), preferred_element_type=F32)


def _const_spec(shape, layer=None):
    if layer is None:
        return pl.BlockSpec(shape, lambda *_: (0,) * len(shape), pipeline_mode=pl.Buffered(1))
    return pl.BlockSpec((None,) + shape, lambda *_: (layer,) + (0,) * len(shape), pipeline_mode=pl.Buffered(1))


def _mod_spec(layer, mod_row):
    row = (lambda b: b) if mod_row is None else (lambda b: mod_row)
    return pl.BlockSpec((None, 6, 1, 1, D_MODEL), lambda b, c: (layer, 0, row(b), 0, 0))


def _mod_kernel(c_ref, w_ref, b_ref, o_ref):
    s = jax.nn.silu(c_ref[...])
    o_ref[0] = jnp.dot(s, w_ref[0], preferred_element_type=F32,
                       precision=lax.Precision.HIGHEST) + b_ref[0]


def _modulation(c_all, w_mod, b_mod):
    depth = w_mod.shape[0]
    n_tiles = 6
    return pl.pallas_call(
        _mod_kernel,
        grid=(depth, n_tiles),
        in_specs=[
            pl.BlockSpec((MOD_ROWS, D_MODEL), lambda i, j: (0, 0)),
            pl.BlockSpec((1, D_MODEL, D_MODEL), lambda i, j: (i, 0, j)),
            pl.BlockSpec((1, 1, D_MODEL), lambda i, j: (i, 0, j)),
        ],
        out_specs=pl.BlockSpec((1, MOD_ROWS, D_MODEL), lambda i, j: (i, 0, j)),
        out_shape=jax.ShapeDtypeStruct((depth, MOD_ROWS, 6 * D_MODEL), F32),
        name="modulation",
    )(c_all, w_mod, b_mod.reshape(depth, 1, 6 * D_MODEL))


def _proj_kernel(*refs, n_chunks, add_pe, ssm_only):
    it = iter(refs)
    x_ref = next(it)
    pe_ref = next(it) if add_pe else None
    mod_ref, nw_ref, win_ref = next(it), next(it), next(it)
    if not ssm_only:
        sguw_ref, sgub_ref, band_ref, icnt_ref, poolw_ref, pscale_ref = (next(it) for _ in range(6))
    u_ref = next(it)
    cat_ref = None if ssm_only else next(it)

    tt = n_chunks * CHUNK
    x = x_ref[0]
    if add_pe:
        x = x + pe_ref[...]
    h = _rms(x, nw_ref[...]) * (1.0 + mod_ref[1, 0]) + mod_ref[0, 0]
    hb = h.astype(BF16)

    zs = _dot_nt(win_ref[S_LO:P_LO, :], hb)
    for r in range(n_chunks):
        u_ref[:, r, :, :] = zs[:, r * CHUNK:(r + 1) * CHUNK].reshape(SSM_GROUPS, SSM_GROUP, CHUNK)
    if ssm_only:
        return

    za = jax.nn.gelu(_dot_nt(win_ref[0:S_LO, :], hb))
    v = za[A_WIDTH:, :].reshape(A_HEADS, A_HEAD_DIM, tt)
    mu = jnp.mean(v, axis=1, keepdims=True)
    vc = v - mu
    var = jnp.mean(vc * vc, axis=1, keepdims=True)
    vn = (vc * lax.rsqrt(var + EPS)).astype(BF16)
    for hd in range(A_HEADS):
        lhs = jnp.concatenate([vn[hd][:, r * CHUNK:(r + 1) * CHUNK] for r in range(n_chunks)], axis=0)
        s = _dot(lhs, sguw_ref[hd]) + sgub_ref[hd]
        for r in range(n_chunks):
            ug = za[hd * A_HEAD_DIM:(hd + 1) * A_HEAD_DIM, r * CHUNK:(r + 1) * CHUNK]
            a = ug * s[r * A_HEAD_DIM:(r + 1) * A_HEAD_DIM, :]
            cat_ref[0, r, hd * A_HEAD_DIM:(hd + 1) * A_HEAD_DIM, :] = a.astype(BF16)

    zp = _dot_nt(win_ref[P_LO:D_IN, :], hb)
    n_seg = tt // POOL_SEG
    diffs = []
    for i in range(len(POOL_WINDOWS)):
        pg = zp[i * POOL_GROUP:(i + 1) * POOL_GROUP, :]
        lhs = jnp.concatenate([pg[:, j * POOL_SEG:(j + 1) * POOL_SEG] for j in range(n_seg)], axis=0)
        m = _dot(lhs.astype(BF16), band_ref[i]) * icnt_ref[i]
        m = jnp.concatenate([m[j * POOL_GROUP:(j + 1) * POOL_GROUP, :] for j in range(n_seg)], axis=1)
        diffs.append(m - pg)
    dt = jnp.concatenate(diffs, axis=0).astype(BF16)
    pool = _dot(poolw_ref[...], dt) * pscale_ref[...]
    for r in range(n_chunks):
        cat_ref[0, r, A_WIDTH:, :] = pool[:, r * CHUNK:(r + 1) * CHUNK].astype(BF16)


def _proj_call(x, pe, mod_all, layer, mod_row, prep, pool_tables, *, n_chunks, ssm_only):
    bsz, seq, _ = x.shape
    tt = n_chunks * CHUNK
    chunks = seq // CHUNK
    add_pe = pe is not None

    in_specs = [pl.BlockSpec((1, tt, D_MODEL), lambda b, c: (b, c, 0))]
    args = [x]
    if add_pe:
        in_specs.append(pl.BlockSpec((tt, D_MODEL), lambda b, c: (c, 0)))
        args.append(pe)
    in_specs += [_mod_spec(layer, mod_row), _const_spec((1, D_MODEL), layer), _const_spec((D_IN, D_MODEL), layer)]
    args += [mod_all, prep["nw_mix_pre"], prep["win_t"]]
    if not ssm_only:
        band, icnt = pool_tables
        in_specs += [
            _const_spec((A_HEADS, CHUNK, CHUNK), layer),
            _const_spec((A_HEADS, 1, CHUNK), layer),
            _const_spec((len(POOL_WINDOWS), POOL_SEG, POOL_SEG)),
            _const_spec((len(POOL_WINDOWS), 1, POOL_SEG)),
            _const_spec((C_WIDTH, C_WIDTH), layer),
            _const_spec((C_WIDTH, 1), layer),
        ]
        args += [prep["sguw_t"], prep["sgub"], band, icnt, prep["poolw_t"], prep["pscale"]]

    u_shape = jax.ShapeDtypeStruct((SSM_GROUPS, chunks, bsz * SSM_GROUP, CHUNK), F32)
    u_spec = pl.BlockSpec((SSM_GROUPS, n_chunks, SSM_GROUP, CHUNK), lambda b, c: (0, c, b, 0))
    if ssm_only:
        out_shape, out_specs = u_shape, u_spec
    else:
        out_shape = (u_shape, jax.ShapeDtypeStruct((bsz, chunks, 2 * A_WIDTH, CHUNK), BF16))
        out_specs = (u_spec, pl.BlockSpec((1, n_chunks, 2 * A_WIDTH, CHUNK), lambda b, c: (b, c, 0, 0)))

    return pl.pallas_call(
        functools.partial(_proj_kernel, n_chunks=n_chunks, add_pe=add_pe, ssm_only=ssm_only),
        grid=(bsz, seq // tt),
        in_specs=in_specs,
        out_specs=out_specs,
        out_shape=out_shape,
        compiler_params=pltpu.CompilerParams(
            dimension_semantics=("parallel", "parallel"), vmem_limit_bytes=V7X_VMEM_LIMIT),
        name="proj_ssm_only" if ssm_only else "proj_local_mix",
    )(*args)


def _toeplitz_kernel(r_ref, o_ref):
    def per_in_channel(hh, carry):
        rows = pl.ds(pl.multiple_of(hh * CHUNK, CHUNK), CHUNK)
        for oo in range(SSM_GROUP):
            table = jnp.broadcast_to(r_ref[0, hh, oo:oo + 1, :], (CHUNK, 2 * CHUNK))
            shifted = pltpu.roll(table, 0, 1, stride=1, stride_axis=0)
            o_ref[0, rows, oo * CHUNK:(oo + 1) * CHUNK] = shifted[:, CHUNK:].astype(BF16)
        return carry

    lax.fori_loop(0, SSM_GROUP, per_in_channel, 0)


def _toeplitz_call(lag_table):
    groups = lag_table.shape[0]
    return pl.pallas_call(
        _toeplitz_kernel,
        grid=(groups,),
        in_specs=[pl.BlockSpec((1, SSM_GROUP, SSM_GROUP, 2 * CHUNK), lambda g: (g, 0, 0, 0))],
        out_specs=pl.BlockSpec((1, SSM_WIDE, SSM_WIDE), lambda g: (g, 0, 0)),
        out_shape=jax.ShapeDtypeStruct((groups, SSM_WIDE, SSM_WIDE), BF16),
        compiler_params=pltpu.CompilerParams(
            dimension_semantics=("parallel",), vmem_limit_bytes=V7X_VMEM_LIMIT),
        name="ssm_toeplitz_build",
    )(lag_table)


def _ssm_kernel(*refs, rows_lat, rows_ctx, bsz, ctx_out):
    if ctx_out:
        ul_ref, uc_ref, kmat_ref, e_ref, min_ref, lam_ref, d_ref, yl_ref, yc_ref, lhs_ref, sin_ref = refs
    else:
        ul_ref, uc_ref, kmat_ref, e_ref, min_ref, lam_ref, d_ref, yl_ref, lhs_ref, sin_ref = refs
        yc_ref = None
    n = pl.program_id(1)
    n_lat = rows_lat // bsz
    n_ctx = rows_ctx // bsz
    half = 2 * SSM_STATE

    @pl.when(n == 0)
    def _():
        for hh in range(SSM_GROUP):
            lanes = slice(hh * CHUNK, (hh + 1) * CHUNK)
            lhs_ref[0:rows_lat, lanes] = ul_ref[0, pl.ds(hh, rows_lat, stride=SSM_GROUP), :].astype(BF16)
            lhs_ref[rows_lat:, lanes] = uc_ref[0, pl.ds(hh, rows_ctx, stride=SSM_GROUP), :].astype(BF16)
        sloc = _dot(lhs_ref[...], e_ref[0])
        lam_re = lam_ref[0, :, 0:half]
        lam_im = lam_ref[0, :, half:]
        fwd_lane = lax.broadcasted_iota(jnp.int32, (bsz, half), 1) < SSM_STATE

        def step(state, add):
            s_re, s_im = state
            a_re, a_im = add
            return (lam_re * s_re - lam_im * s_im + a_re, lam_re * s_im + lam_im * s_re + a_im)

        def rows_of(base, c):
            blk = sloc[base + c * bsz:base + (c + 1) * bsz, :]
            return blk[:, 0:half], blk[:, half:]

        def put(base, c, f_state, b_state):
            sin_ref[base + c * bsz:base + (c + 1) * bsz, 0:half] = jnp.where(
                fwd_lane, f_state[0], b_state[0]).astype(BF16)
            sin_ref[base + c * bsz:base + (c + 1) * bsz, half:] = jnp.where(
                fwd_lane, f_state[1], b_state[1]).astype(BF16)

        def sweep(base, count, f0, b0):
            f_in, b_in = [None] * count, [None] * count
            f, b = f0, b0
            for c in range(count):
                f_in[c] = f
                f = step(f, rows_of(base, c))
            f_end = f
            for c in reversed(range(count)):
                b_in[c] = b
                b = step(b, rows_of(base, c))
            for c in range(count):
                put(base, c, f_in[c], b_in[c])
            return f_end, b

        zero = (jnp.zeros((bsz, half), F32), jnp.zeros((bsz, half), F32))
        f_end, b_end = sweep(rows_lat, n_ctx, zero, zero)
        sweep(0, n_lat, f_end, b_end)

    n_cols = kmat_ref.shape[2]
    pair = 2 * CHUNK
    for q in range(n_cols // pair):
        cols = slice(q * pair, (q + 1) * pair)
        y = _dot(lhs_ref[...], kmat_ref[0, :, cols]) + _dot(sin_ref[...], min_ref[0, :, cols])
        for k in range(2):
            hh = n * (n_cols // CHUNK) + 2 * q + k
            piece = y[:, k * CHUNK:(k + 1) * CHUNK]
            skip = d_ref[0, pl.ds(hh, 1), :]
            lat_rows = pl.ds(hh, rows_lat, stride=SSM_GROUP)
            yl_ref[0, lat_rows, :] = piece[0:rows_lat] + skip * ul_ref[0, lat_rows, :]
            if ctx_out:
                ctx_rows = pl.ds(hh, rows_ctx, stride=SSM_GROUP)
                yc_ref[0, ctx_rows, :] = piece[rows_lat:] + skip * uc_ref[0, ctx_rows, :]


def _ssm_call(u_lat, u_ctx, ops, layer, *, ctx_out):
    groups, c_lat = u_lat.shape[:2]
    c_ctx = u_ctx.shape[1]
    bsz = u_lat.shape[2] // SSM_GROUP
    rows_lat, rows_ctx = c_lat * bsz, c_ctx * bsz
    rows = rows_lat + rows_ctx
    n_split = 2
    off = layer * groups
    ul = u_lat.reshape(groups, rows_lat * SSM_GROUP, CHUNK)
    uc = u_ctx.reshape(groups, rows_ctx * SSM_GROUP, CHUNK)
    per_g = lambda g, n: (g, 0, 0)
    per_lg = lambda g, n: (g + off, 0, 0)
    per_lgn = lambda g, n: (g + off, 0, n)
    out_shape = [jax.ShapeDtypeStruct(ul.shape, F32)]
    out_specs = [pl.BlockSpec((1, rows_lat * SSM_GROUP, CHUNK), per_g)]
    if ctx_out:
        out_shape.append(jax.ShapeDtypeStruct(uc.shape, F32))
        out_specs.append(pl.BlockSpec((1, rows_ctx * SSM_GROUP, CHUNK), per_g))
    outs = pl.pallas_call(
        functools.partial(_ssm_kernel, rows_lat=rows_lat, rows_ctx=rows_ctx, bsz=bsz, ctx_out=ctx_out),
        grid=(groups, n_split),
        in_specs=[
            pl.BlockSpec((1, rows_lat * SSM_GROUP, CHUNK), per_g),
            pl.BlockSpec((1, rows_ctx * SSM_GROUP, CHUNK), per_g),
            pl.BlockSpec((1, SSM_WIDE, SSM_WIDE // n_split), per_lgn),
            pl.BlockSpec((1, SSM_WIDE, 4 * SSM_STATE), per_lg),
            pl.BlockSpec((1, 4 * SSM_STATE, SSM_WIDE // n_split), per_lgn),
            pl.BlockSpec((1, 1, 4 * SSM_STATE), per_lg),
            pl.BlockSpec((1, SSM_GROUP, CHUNK), per_lg),
        ],
        out_specs=out_specs,
        out_shape=out_shape,
        scratch_shapes=[pltpu.VMEM((rows, SSM_WIDE), BF16), pltpu.VMEM((rows, 4 * SSM_STATE), BF16)],
        compiler_params=pltpu.CompilerParams(
            dimension_semantics=("parallel", "arbitrary"), vmem_limit_bytes=V7X_VMEM_LIMIT),
        name="ssm_chunked_scan",
    )(ul, uc, ops["kmat"], ops["e_mat"], ops["m_in"], ops["lam_t"], ops["d_rows"])
    y_lat = outs[0].reshape(u_lat.shape)
    y_ctx = outs[1].reshape(u_ctx.shape) if ctx_out else None
    return y_lat, y_ctx


def _mix_ffn_kernel(*refs, n_chunks, add_pe):
    it = iter(refs)
    x_ref = next(it)
    pe_ref = next(it) if add_pe else None
    (y_ref, cat_ref, mod_ref, nw_mix_ref, nw_pre_ref, nw_post_ref, gluw_ref, glub_ref, wout_ref,
     wg_ref, wu_ref, wd_ref, o_ref, x1_ref, acc_ref) = (next(it) for _ in range(15))

    yt = jnp.concatenate([y_ref[:, r, :, :].reshape(B_WIDTH, CHUNK) for r in range(n_chunks)], axis=1)
    g = jax.nn.gelu(yt)
    gate = jax.nn.sigmoid(_dot(gluw_ref[...], g.astype(BF16)) + glub_ref[...])
    st = (g * gate).astype(BF16)
    at = jnp.concatenate([cat_ref[0, r, 0:A_WIDTH, :] for r in range(n_chunks)], axis=1)
    pt = jnp.concatenate([cat_ref[0, r, A_WIDTH:, :] for r in range(n_chunks)], axis=1)
    cat_t = jnp.concatenate([at, st, pt], axis=0)
    m = _dot_tn(cat_t, wout_ref[...])
    x = x_ref[0]
    if add_pe:
        x = x + pe_ref[...]
    x1 = x + mod_ref[2, 0] * _rms(m, nw_mix_ref[...])
    x1_ref[...] = x1

    h = (_rms(x1, nw_pre_ref[...]) * (1.0 + mod_ref[4, 0]) + mod_ref[3, 0]).astype(BF16)
    for k in range(N_FF_CHUNKS):
        act = (jax.nn.silu(_dot(h, wg_ref[k])) * _dot(h, wu_ref[k])).astype(BF16)
        part = _dot(act, wd_ref[k])
        if k == 0:
            acc_ref[...] = part
        else:
            acc_ref[...] += part
    o_ref[0] = x1_ref[...] + mod_ref[5, 0] * _rms(acc_ref[...], nw_post_ref[...])


def _mix_ffn_call(x, pe, y, cat, mod_all, layer, mod_row, prep, *, n_chunks):
    bsz, seq, _ = x.shape
    tt = n_chunks * CHUNK
    add_pe = pe is not None
    in_specs = [pl.BlockSpec((1, tt, D_MODEL), lambda b, c: (b, c, 0))]
    args = [x]
    if add_pe:
        in_specs.append(pl.BlockSpec((tt, D_MODEL), lambda b, c: (c, 0)))
        args.append(pe)
    in_specs += [
        pl.BlockSpec((SSM_GROUPS, n_chunks, SSM_GROUP, CHUNK), lambda b, c: (0, c, b, 0)),
        pl.BlockSpec((1, n_chunks, 2 * A_WIDTH, CHUNK), lambda b, c: (b, c, 0, 0)),
        _mod_spec(layer, mod_row),
        _const_spec((1, D_MODEL), layer),
        _const_spec((1, D_MODEL), layer),
        _const_spec((1, D_MODEL), layer),
        _const_spec((B_WIDTH, B_WIDTH), layer),
        _const_spec((B_WIDTH, 1), layer),
        _const_spec((D_MODEL, D_MODEL), layer),
        _const_spec((N_FF_CHUNKS, D_MODEL, FF_CHUNK), layer),
        _const_spec((N_FF_CHUNKS, D_MODEL, FF_CHUNK), layer),
        _const_spec((N_FF_CHUNKS, FF_CHUNK, D_MODEL), layer),
    ]
    args += [y, cat, mod_all, prep["nw_mix_post"], prep["nw_ffn_pre"], prep["nw_ffn_post"], prep["gluw_t"],
             prep["glub_col"], prep["wout"], prep["wg"], prep["wu"], prep["wd"]]
    return pl.pallas_call(
        functools.partial(_mix_ffn_kernel, n_chunks=n_chunks, add_pe=add_pe),
        grid=(bsz, seq // tt),
        in_specs=in_specs,
        out_specs=pl.BlockSpec((1, tt, D_MODEL), lambda b, c: (b, c, 0)),
        out_shape=jax.ShapeDtypeStruct(x.shape, F32),
        scratch_shapes=[pltpu.VMEM((tt, D_MODEL), F32), pltpu.VMEM((tt, D_MODEL), F32)],
        compiler_params=pltpu.CompilerParams(
            dimension_semantics=("parallel", "parallel"), vmem_limit_bytes=V7X_VMEM_LIMIT),
        name="mix_out_ffn",
    )(*args)


def _sincos_2d(rows, cols, dim):
    quarter = dim // 4
    omega = 1.0 / (10000.0 ** (jnp.arange(quarter, dtype=F32) / quarter))
    r = jnp.arange(rows, dtype=F32)[:, None] * omega
    cc = jnp.arange(cols, dtype=F32)[:, None] * omega
    er = jnp.concatenate([jnp.sin(r), jnp.cos(r)], axis=-1)
    ec = jnp.concatenate([jnp.sin(cc), jnp.cos(cc)], axis=-1)
    pe = jnp.concatenate([jnp.broadcast_to(er[:, None, :], (rows, cols, dim // 2)),
                          jnp.broadcast_to(ec[None, :, :], (rows, cols, dim // 2))], axis=-1)
    return pe.reshape(rows * cols, dim)


def _pool_bands(row_len):
    pos = np.arange(POOL_SEG)
    start = (pos // row_len) * row_len
    t = pos - start
    bands, icnts = [], []
    for w in POOL_WINDOWS:
        lo = np.clip(t - w // 2, 0, row_len) + start
        hi = np.clip(t - w // 2 + w, 0, row_len) + start
        bands.append(((pos[:, None] >= lo[None, :]) & (pos[:, None] < hi[None, :])).astype(np.float32))
        icnts.append((1.0 / (hi - lo).astype(np.float32))[None, :])
    return jnp.asarray(np.stack(bands), BF16), jnp.asarray(np.stack(icnts), F32)


def _ssm_operators(lam_re, lam_im, log_dt, b_re, b_im, c_re, c_im, d):
    t_len = CHUNK
    groups = lam_re.shape[1]
    dt = jnp.exp(log_dt)[..., None]
    a_re, a_im = lam_re * dt, lam_im * dt
    mag = jnp.exp(a_re)
    lb_re, lb_im = mag * jnp.cos(a_im), mag * jnp.sin(a_im)
    den = lam_re * lam_re + lam_im * lam_im
    q_re = ((lb_re - 1.0) * lam_re + lb_im * lam_im) / den
    q_im = (lb_im * lam_re - (lb_re - 1.0) * lam_im) / den
    bb_re = q_re[..., None] * b_re - q_im[..., None] * b_im
    bb_im = q_re[..., None] * b_im + q_im[..., None] * b_re
    k = jnp.arange(t_len + 1, dtype=F32)
    p_mag = jnp.exp(a_re[..., None] * k)
    p_re = p_mag * jnp.cos(a_im[..., None] * k)
    p_im = p_mag * jnp.sin(a_im[..., None] * k)

    w_re = c_re[..., None] * p_re[:, :, None] - c_im[..., None] * p_im[:, :, None]
    w_im = c_re[..., None] * p_im[:, :, None] + c_im[..., None] * p_re[:, :, None]
    hp = lax.Precision.HIGHEST
    resp = (jnp.einsum('dgopk,dgph->dghok', w_re[..., :t_len], bb_re, precision=hp)
            - jnp.einsum('dgopk,dgph->dghok', w_im[..., :t_len], bb_im, precision=hp))
    r_f, r_b = resp[0], resp[1]
    lag_table = jnp.concatenate([jnp.zeros_like(r_f[..., :1]), r_b[..., :0:-1],
                                 r_f[..., :1] + r_b[..., :1], r_f[..., 1:]], axis=-1)
    kmat = _toeplitz_call(lag_table)

    def end_map(pw_re, pw_im, dr):
        br, bi = bb_re[dr][:, :, :, None], bb_im[dr][:, :, :, None]
        pr, pi = pw_re[:, :, None, :], pw_im[:, :, None, :]
        return jnp.transpose(br * pr - bi * pi, (0, 2, 3, 1)), jnp.transpose(br * pi + bi * pr, (0, 2, 3, 1))

    ef_re, ef_im = end_map(p_re[0][..., t_len - 1::-1], p_im[0][..., t_len - 1::-1], 0)
    eb_re, eb_im = end_map(p_re[1][..., :t_len], p_im[1][..., :t_len], 1)
    e_mat = jnp.concatenate([ef_re, eb_re, ef_im, eb_im], axis=-1).reshape(groups, SSM_WIDE, 4 * SSM_STATE)

    def in_map(wr, wi):
        return jnp.transpose(wr, (0, 2, 1, 3)), -jnp.transpose(wi, (0, 2, 1, 3))

    mf_re, mf_im = in_map(w_re[0][..., 1:], w_im[0][..., 1:])
    mb_re, mb_im = in_map(w_re[1][..., :0:-1], w_im[1][..., :0:-1])
    m_in = jnp.concatenate([mf_re, mb_re, mf_im, mb_im], axis=1).reshape(groups, 4 * SSM_STATE, SSM_WIDE)

    lam_t = jnp.concatenate([p_re[0][..., t_len], p_re[1][..., t_len],
                             p_im[0][..., t_len], p_im[1][..., t_len]], axis=-1)[:, None, :]
    d_rows = jnp.broadcast_to(d[:, :, None], (groups, SSM_GROUP, CHUNK))
    return {"kmat": kmat, "e_mat": e_mat.astype(BF16), "m_in": m_in.astype(BF16), "lam_t": lam_t, "d_rows": d_rows}


def _fold_layers(p):
    p = jnp.swapaxes(p, 0, 1)
    return p.reshape((2, p.shape[1] * p.shape[2]) + p.shape[3:])


def _block_diag_t(blocks):
    depth, n, k, _ = blocks.shape
    eye = jnp.eye(n, dtype=blocks.dtype)
    out = jnp.einsum('dnab,nm->dnbma', blocks, eye)
    return out.reshape(depth, n * k, n * k)


def kernel(x, c, ctx, c_ctx, w_mod, b_mod, norm_mix_pre, norm_mix_post, norm_ffn_pre, norm_ffn_post, w_in, w_out, sgu_w, sgu_b, ssm_lam_re, ssm_lam_im, ssm_log_dt, ssm_b_re, ssm_b_im, ssm_c_re, ssm_c_im, ssm_d, glu_w, glu_b, pool_w, pool_scale, ffn_w_gate, ffn_w_up, ffn_w_down):
    bsz, n_lat, _ = x.shape
    n_ctx = ctx.shape[1]
    depth = w_mod.shape[0]
    assert bsz + 1 <= MOD_ROWS and n_lat % (8 * CHUNK) == 0 and n_ctx == POOL_SEG

    pe = _sincos_2d(n_lat // GRID_W, GRID_W, D_MODEL)
    c_all = jnp.concatenate([c, c_ctx[None, :], jnp.zeros((MOD_ROWS - bsz - 1, D_MODEL), F32)], axis=0)
    mod_all = _modulation(c_all, w_mod, b_mod)
    mod_all = jnp.transpose(mod_all.reshape(depth, MOD_ROWS, 6, 1, D_MODEL), (0, 2, 1, 3, 4))
    ctx_row = bsz

    band_lat = _pool_bands(GRID_W)
    band_ctx = _pool_bands(n_ctx)

    prep = {
        "nw_mix_pre": norm_mix_pre[:, None, :], "nw_mix_post": norm_mix_post[:, None, :],
        "nw_ffn_pre": norm_ffn_pre[:, None, :], "nw_ffn_post": norm_ffn_post[:, None, :],
        "win_t": jnp.swapaxes(w_in, 1, 2).astype(BF16),
        "sguw_t": jnp.swapaxes(sgu_w, 2, 3).astype(BF16),
        "sgub": sgu_b[:, :, None, :],
        "poolw_t": _block_diag_t(pool_w).astype(BF16),
        "pscale": pool_scale[:, :, None],
        "gluw_t": jnp.swapaxes(glu_w, 1, 2).astype(BF16),
        "glub_col": glu_b[:, :, None],
        "wout": w_out.astype(BF16),
        "wg": jnp.transpose(ffn_w_gate.reshape(depth, D_MODEL, N_FF_CHUNKS, FF_CHUNK), (0, 2, 1, 3)).astype(BF16),
        "wu": jnp.transpose(ffn_w_up.reshape(depth, D_MODEL, N_FF_CHUNKS, FF_CHUNK), (0, 2, 1, 3)).astype(BF16),
        "wd": ffn_w_down.reshape(depth, N_FF_CHUNKS, FF_CHUNK, D_MODEL).astype(BF16),
    }
    ops = _ssm_operators(_fold_layers(ssm_lam_re), _fold_layers(ssm_lam_im), _fold_layers(ssm_log_dt),
                         _fold_layers(ssm_b_re), _fold_layers(ssm_b_im), _fold_layers(ssm_c_re),
                         _fold_layers(ssm_c_im), ssm_d.reshape(depth * SSM_GROUPS, SSM_GROUP))

    x_lat, x_ctx = x, ctx
    for i in range(depth):
        need_ctx = i < depth - 1
        pe_i = pe if i == 0 else None
        u_lat, cat_lat = _proj_call(x_lat, pe_i, mod_all, i, None, prep, band_lat, n_chunks=8, ssm_only=False)
        if need_ctx:
            u_ctx, cat_ctx = _proj_call(x_ctx, None, mod_all, i, ctx_row, prep, band_ctx, n_chunks=2, ssm_only=False)
        else:
            u_ctx = _proj_call(x_ctx, None, mod_all, i, ctx_row, prep, None, n_chunks=2, ssm_only=True)
        y_lat, y_ctx = _ssm_call(u_lat, u_ctx, ops, i, ctx_out=need_ctx)
        x_lat = _mix_ffn_call(x_lat, pe_i, y_lat, cat_lat, mod_all, i, None, prep, n_chunks=4)
        if need_ctx:
            x_ctx = _mix_ffn_call(x_ctx, None, y_ctx, cat_ctx, mod_all, i, ctx_row, prep, n_chunks=2)
    return x_lat
```

```python
import functools

import numpy as np
import jax
import jax.numpy as jnp
from jax import lax
from jax.experimental import pallas as pl
from jax.experimental.pallas import tpu as pltpu

D_MODEL = 1024
GRID_W = 64
EPS = 1e-6
A_WIDTH = 256
A_HEADS = 4
A_HEAD_DIM = 64
CHUNK = 128
B_WIDTH = 512
SSM_GROUP = 16
SSM_GROUPS = 32
SSM_STATE = 64
C_WIDTH = 256
POOL_WINDOWS = (2, 4, 8, 16)
POOL_GROUP = 64
D_IN = 1280
D_FF = 2816
S_LO = 2 * A_WIDTH
P_LO = S_LO + B_WIDTH
POOL_SEG = 256
FF_CHUNK = 256
N_FF_CHUNKS = D_FF // FF_CHUNK
SSM_WIDE = SSM_GROUP * CHUNK
MOD_ROWS = 40
V7X_VMEM_LIMIT = 56 * 1024 * 1024

F32 = jnp.float32
BF16 = jnp.bfloat16


def _rms(x, w):
    return x * lax.rsqrt(jnp.mean(x * x, axis=-1, keepdims=True) + EPS) * w


def _dot(a, b):
    return jnp.dot(a, b, preferred_element_type=F32)


def _dot_nt(a, b):
    return lax.dot_general(a, b, (((1,), (1,)), ((), ())), preferred_element_type=F32)


def _dot_tn(a, b):
    return lax.dot_general(a, b, (((0,), (0,)), ((), ())), preferred_element_type=F32)


def _const_spec(shape, layer=None):
    if layer is None:
        return pl.BlockSpec(shape, lambda *_: (0,) * len(shape), pipeline_mode=pl.Buffered(1))
    return pl.BlockSpec((None,) + shape, lambda *_: (layer,) + (0,) * len(shape), pipeline_mode=pl.Buffered(1))


def _mod_spec(layer, mod_row):
    row = (lambda b: b) if mod_row is None else (lambda b: mod_row)
    return pl.BlockSpec((None, 6, 1, 1, D_MODEL), lambda b, c: (layer, 0, row(b), 0, 0))


def _mod_kernel(c_ref, w_ref, b_ref, o_ref):
    s = jax.nn.silu(c_ref[...])
    o_ref[0] = jnp.dot(s, w_ref[0], preferred_element_type=F32,
                       precision=lax.Precision.HIGHEST) + b_ref[0]


def _modulation(c_all, w_mod, b_mod):
    depth = w_mod.shape[0]
    n_tiles = 6
    return pl.pallas_call(
        _mod_kernel,
        grid=(depth, n_tiles),
        in_specs=[
            pl.BlockSpec((MOD_ROWS, D_MODEL), lambda i, j: (0, 0)),
            pl.BlockSpec((1, D_MODEL, D_MODEL), lambda i, j: (i, 0, j)),
            pl.BlockSpec((1, 1, D_MODEL), lambda i, j: (i, 0, j)),
        ],
        out_specs=pl.BlockSpec((1, MOD_ROWS, D_MODEL), lambda i, j: (i, 0, j)),
        out_shape=jax.ShapeDtypeStruct((depth, MOD_ROWS, 6 * D_MODEL), F32),
        name="modulation",
    )(c_all, w_mod, b_mod.reshape(depth, 1, 6 * D_MODEL))


def _proj_kernel(*refs, n_chunks, add_pe, ssm_only):
    it = iter(refs)
    x_ref = next(it)
    pe_ref = next(it) if add_pe else None
    mod_ref, nw_ref, win_ref = next(it), next(it), next(it)
    if not ssm_only:
        sguw_ref, sgub_ref, band_ref, icnt_ref, poolw_ref, pscale_ref = (next(it) for _ in range(6))
    u_ref = next(it)
    cat_ref = None if ssm_only else next(it)

    tt = n_chunks * CHUNK
    x = x_ref[0]
    if add_pe:
        x = x + pe_ref[...]
    h = _rms(x, nw_ref[...]) * (1.0 + mod_ref[1, 0]) + mod_ref[0, 0]
    hb = h.astype(BF16)

    zs = _dot_nt(win_ref[S_LO:P_LO, :], hb)
    for r in range(n_chunks):
        u_ref[:, r, :, :] = zs[:, r * CHUNK:(r + 1) * CHUNK].reshape(SSM_GROUPS, SSM_GROUP, CHUNK)
    if ssm_only:
        return

    za = jax.nn.gelu(_dot_nt(win_ref[0:S_LO, :], hb))
    v = za[A_WIDTH:, :].reshape(A_HEADS, A_HEAD_DIM, tt)
    mu = jnp.mean(v, axis=1, keepdims=True)
    vc = v - mu
    var = jnp.mean(vc * vc, axis=1, keepdims=True)
    vn = (vc * lax.rsqrt(var + EPS)).astype(BF16)
    for hd in range(A_HEADS):
        lhs = jnp.concatenate([vn[hd][:, r * CHUNK:(r + 1) * CHUNK] for r in range(n_chunks)], axis=0)
        s = _dot(lhs, sguw_ref[hd]) + sgub_ref[hd]
        for r in range(n_chunks):
            ug = za[hd * A_HEAD_DIM:(hd + 1) * A_HEAD_DIM, r * CHUNK:(r + 1) * CHUNK]
            a = ug * s[r * A_HEAD_DIM:(r + 1) * A_HEAD_DIM, :]
            cat_ref[0, r, hd * A_HEAD_DIM:(hd + 1) * A_HEAD_DIM, :] = a.astype(BF16)

    zp = _dot_nt(win_ref[P_LO:D_IN, :], hb)
    n_seg = tt // POOL_SEG
    diffs = []
    for i in range(len(POOL_WINDOWS)):
        pg = zp[i * POOL_GROUP:(i + 1) * POOL_GROUP, :]
        lhs = jnp.concatenate([pg[:, j * POOL_SEG:(j + 1) * POOL_SEG] for j in range(n_seg)], axis=0)
        m = _dot(lhs.astype(BF16), band_ref[i]) * icnt_ref[i]
        m = jnp.concatenate([m[j * POOL_GROUP:(j + 1) * POOL_GROUP, :] for j in range(n_seg)], axis=1)
        diffs.append(m - pg)
    dt = jnp.concatenate(diffs, axis=0).astype(BF16)
    pool = _dot(poolw_ref[...], dt) * pscale_ref[...]
    for r in range(n_chunks):
        cat_ref[0, r, A_WIDTH:, :] = pool[:, r * CHUNK:(r + 1) * CHUNK].astype(BF16)


def _proj_call(x, pe, mod_all, layer, mod_row, prep, pool_tables, *, n_chunks, ssm_only):
    bsz, seq, _ = x.shape
    tt = n_chunks * CHUNK
    chunks = seq // CHUNK
    add_pe = pe is not None

    in_specs = [pl.BlockSpec((1, tt, D_MODEL), lambda b, c: (b, c, 0))]
    args = [x]
    if add_pe:
        in_specs.append(pl.BlockSpec((tt, D_MODEL), lambda b, c: (c, 0)))
        args.append(pe)
    in_specs += [_mod_spec(layer, mod_row), _const_spec((1, D_MODEL), layer), _const_spec((D_IN, D_MODEL), layer)]
    args += [mod_all, prep["nw_mix_pre"], prep["win_t"]]
    if not ssm_only:
        band, icnt = pool_tables
        in_specs += [
            _const_spec((A_HEADS, CHUNK, CHUNK), layer),
            _const_spec((A_HEADS, 1, CHUNK), layer),
            _const_spec((len(POOL_WINDOWS), POOL_SEG, POOL_SEG)),
            _const_spec((len(POOL_WINDOWS), 1, POOL_SEG)),
            _const_spec((C_WIDTH, C_WIDTH), layer),
            _const_spec((C_WIDTH, 1), layer),
        ]
        args += [prep["sguw_t"], prep["sgub"], band, icnt, prep["poolw_t"], prep["pscale"]]

    u_shape = jax.ShapeDtypeStruct((SSM_GROUPS, chunks, bsz * SSM_GROUP, CHUNK), F32)
    u_spec = pl.BlockSpec((SSM_GROUPS, n_chunks, SSM_GROUP, CHUNK), lambda b, c: (0, c, b, 0))
    if ssm_only:
        out_shape, out_specs = u_shape, u_spec
    else:
        out_shape = (u_shape, jax.ShapeDtypeStruct((bsz, chunks, 2 * A_WIDTH, CHUNK), BF16))
        out_specs = (u_spec, pl.BlockSpec((1, n_chunks, 2 * A_WIDTH, CHUNK), lambda b, c: (b, c, 0, 0)))

    return pl.pallas_call(
        functools.partial(_proj_kernel, n_chunks=n_chunks, add_pe=add_pe, ssm_only=ssm_only),
        grid=(bsz, seq // tt),
        in_specs=in_specs,
        out_specs=out_specs,
        out_shape=out_shape,
        compiler_params=pltpu.CompilerParams(
            dimension_semantics=("parallel", "parallel"), vmem_limit_bytes=V7X_VMEM_LIMIT),
        name="proj_ssm_only" if ssm_only else "proj_local_mix",
    )(*args)


def _toeplitz_block(lag_ref, k_ref, hh, oo, rows):
    table = jnp.broadcast_to(lag_ref[0, hh, oo:oo + 1, :], (CHUNK, 2 * CHUNK))
    shifted = pltpu.roll(table, 0, 1, stride=1, stride_axis=0)
    k_ref[rows, oo * CHUNK:(oo + 1) * CHUNK] = shifted[:, CHUNK:].astype(BF16)


def _ssm_group(ul_ref, uc_ref, lag_next_ref, pe_ref, pm_ref, bb_ref, cc_ref, lam_ref, d_ref, yl_ref, yc_ref,
               k_now_ref, k_next_ref, e_ref, min_ref, lhs_ref, sin_ref, *, rows_lat, rows_ctx, bsz):
    n_lat = rows_lat // bsz
    n_ctx = rows_ctx // bsz
    half = 2 * SSM_STATE

    p_re, p_im = pe_ref[0, 0], pe_ref[0, 1]
    q_re, q_im = pm_ref[0, 0], pm_ref[0, 1]
    for hh in range(SSM_GROUP):
        rows = slice(hh * CHUNK, (hh + 1) * CHUNK)
        b_re, b_im = bb_ref[0, 0, hh:hh + 1, :], bb_ref[0, 1, hh:hh + 1, :]
        e_ref[rows, 0:half] = (b_re * p_re - b_im * p_im).astype(BF16)
        e_ref[rows, half:] = (b_re * p_im + b_im * p_re).astype(BF16)
        c_re, c_im = cc_ref[0, 0, :, hh:hh + 1], cc_ref[0, 1, :, hh:hh + 1]
        min_ref[0:half, rows] = (c_re * q_re - c_im * q_im).astype(BF16)
        min_ref[half:, rows] = (-(c_re * q_im + c_im * q_re)).astype(BF16)

    for hh in range(SSM_GROUP):
        lanes = slice(hh * CHUNK, (hh + 1) * CHUNK)
        lhs_ref[0:rows_lat, lanes] = ul_ref[0, pl.ds(hh, rows_lat, stride=SSM_GROUP), :].astype(BF16)
        lhs_ref[rows_lat:, lanes] = uc_ref[0, pl.ds(hh, rows_ctx, stride=SSM_GROUP), :].astype(BF16)
    sloc = _dot(lhs_ref[...], e_ref[...])
    lam_re = lam_ref[0, :, 0:half]
    lam_im = lam_ref[0, :, half:]
    fwd_lane = lax.broadcasted_iota(jnp.int32, (bsz, half), 1) < SSM_STATE

    def step(state, add):
        s_re, s_im = state
        a_re, a_im = add
        return (lam_re * s_re - lam_im * s_im + a_re, lam_re * s_im + lam_im * s_re + a_im)

    def rows_of(base, c):
        blk = sloc[base + c * bsz:base + (c + 1) * bsz, :]
        return blk[:, 0:half], blk[:, half:]

    def put(base, c, f_state, b_state):
        sin_ref[base + c * bsz:base + (c + 1) * bsz, 0:half] = jnp.where(
            fwd_lane, f_state[0], b_state[0]).astype(BF16)
        sin_ref[base + c * bsz:base + (c + 1) * bsz, half:] = jnp.where(
            fwd_lane, f_state[1], b_state[1]).astype(BF16)

    def sweep(base, count, f0, b0):
        f_in, b_in = [None] * count, [None] * count
        f, b = f0, b0
        for c in range(count):
            f_in[c] = f
            f = step(f, rows_of(base, c))
        f_end = f
        for c in reversed(range(count)):
            b_in[c] = b
            b = step(b, rows_of(base, c))
        for c in range(count):
            put(base, c, f_in[c], b_in[c])
        return f_end, b

    zero = (jnp.zeros((bsz, half), F32), jnp.zeros((bsz, half), F32))
    f_end, b_end = sweep(rows_lat, n_ctx, zero, zero)
    sweep(0, n_lat, f_end, b_end)

    pair = 2 * CHUNK
    for q in range(SSM_WIDE // pair):
        cols = slice(q * pair, (q + 1) * pair)
        y = _dot(lhs_ref[...], k_now_ref[:, cols]) + _dot(sin_ref[...], min_ref[:, cols])
        for k in range(2):
            hh = 2 * q + k
            piece = y[:, k * CHUNK:(k + 1) * CHUNK]
            skip = d_ref[0, hh:hh + 1, :]
            lat_rows = pl.ds(hh, rows_lat, stride=SSM_GROUP)
            yl_ref[0, lat_rows, :] = piece[0:rows_lat] + skip * ul_ref[0, lat_rows, :]
            if yc_ref is not None:
                ctx_rows = pl.ds(hh, rows_ctx, stride=SSM_GROUP)
                yc_ref[0, ctx_rows, :] = piece[rows_lat:] + skip * uc_ref[0, ctx_rows, :]
            for oo in range(SSM_GROUP):
                _toeplitz_block(lag_next_ref, k_next_ref, hh, oo, slice(hh * CHUNK, (hh + 1) * CHUNK))


def _ssm_kernel(*refs, rows_lat, rows_ctx, bsz, ctx_out):
    it = iter(refs)
    ul_ref, uc_ref, lag_now_ref, lag_next_ref, pe_ref, pm_ref, bb_ref, cc_ref, lam_ref, d_ref, yl_ref = (
        next(it) for _ in range(11))
    yc_ref = next(it) if ctx_out else None
    k_even_ref, k_odd_ref, e_ref, min_ref, lhs_ref, sin_ref = (next(it) for _ in range(6))
    g = pl.program_id(0)

    @pl.when(g == 0)
    def _():
        def per_in_channel(hh, carry):
            rows = pl.ds(pl.multiple_of(hh * CHUNK, CHUNK), CHUNK)
            for oo in range(SSM_GROUP):
                _toeplitz_block(lag_now_ref, k_even_ref, hh, oo, rows)
            return carry

        lax.fori_loop(0, SSM_GROUP, per_in_channel, 0)

    group = functools.partial(_ssm_group, ul_ref, uc_ref, lag_next_ref, pe_ref, pm_ref, bb_ref, cc_ref, lam_ref,
                              d_ref, yl_ref, yc_ref, e_ref=e_ref, min_ref=min_ref, lhs_ref=lhs_ref,
                              sin_ref=sin_ref, rows_lat=rows_lat, rows_ctx=rows_ctx, bsz=bsz)

    @pl.when(g % 2 == 0)
    def _():
        group(k_now_ref=k_even_ref, k_next_ref=k_odd_ref)

    @pl.when(g % 2 == 1)
    def _():
        group(k_now_ref=k_odd_ref, k_next_ref=k_even_ref)


def _ssm_call(u_lat, u_ctx, ops, layer, *, ctx_out):
    groups, c_lat = u_lat.shape[:2]
    c_ctx = u_ctx.shape[1]
    bsz = u_lat.shape[2] // SSM_GROUP
    rows_lat, rows_ctx = c_lat * bsz, c_ctx * bsz
    rows = rows_lat + rows_ctx
    off = layer * groups
    state_w = 4 * SSM_STATE
    ul = u_lat.reshape(groups, rows_lat * SSM_GROUP, CHUNK)
    uc = u_ctx.reshape(groups, rows_ctx * SSM_GROUP, CHUNK)
    per_g = lambda g: (g, 0, 0)
    per_lg3 = lambda g: (g + off, 0, 0)
    per_lg4 = lambda g: (g + off, 0, 0, 0)
    next_lg4 = lambda g: (jnp.minimum(g + 1, groups - 1) + off, 0, 0, 0)
    out_shape = [jax.ShapeDtypeStruct(ul.shape, F32)]
    out_specs = [pl.BlockSpec((1, rows_lat * SSM_GROUP, CHUNK), per_g)]
    if ctx_out:
        out_shape.append(jax.ShapeDtypeStruct(uc.shape, F32))
        out_specs.append(pl.BlockSpec((1, rows_ctx * SSM_GROUP, CHUNK), per_g))
    outs = pl.pallas_call(
        functools.partial(_ssm_kernel, rows_lat=rows_lat, rows_ctx=rows_ctx, bsz=bsz, ctx_out=ctx_out),
        grid=(groups,),
        in_specs=[
            pl.BlockSpec((1, rows_lat * SSM_GROUP, CHUNK), per_g),
            pl.BlockSpec((1, rows_ctx * SSM_GROUP, CHUNK), per_g),
            pl.BlockSpec((1, SSM_GROUP, SSM_GROUP, 2 * CHUNK), per_lg4),
            pl.BlockSpec((1, SSM_GROUP, SSM_GROUP, 2 * CHUNK), next_lg4),
            pl.BlockSpec((1, 2, CHUNK, 2 * SSM_STATE), per_lg4),
            pl.BlockSpec((1, 2, 2 * SSM_STATE, CHUNK), per_lg4),
            pl.BlockSpec((1, 2, SSM_GROUP, 2 * SSM_STATE), per_lg4),
            pl.BlockSpec((1, 2, 2 * SSM_STATE, SSM_GROUP), per_lg4),
            pl.BlockSpec((1, 1, state_w), per_lg3),
            pl.BlockSpec((1, SSM_GROUP, CHUNK), per_lg3),
        ],
        out_specs=out_specs,
        out_shape=out_shape,
        scratch_shapes=[
            pltpu.VMEM((SSM_WIDE, SSM_WIDE), BF16), pltpu.VMEM((SSM_WIDE, SSM_WIDE), BF16),
            pltpu.VMEM((SSM_WIDE, state_w), BF16), pltpu.VMEM((state_w, SSM_WIDE), BF16),
            pltpu.VMEM((rows, SSM_WIDE), BF16), pltpu.VMEM((rows, state_w), BF16),
        ],
        compiler_params=pltpu.CompilerParams(
            dimension_semantics=("arbitrary",), vmem_limit_bytes=V7X_VMEM_LIMIT),
        name="ssm_chunked_scan",
    )(ul, uc, ops["lag_table"], ops["lag_table"], ops["pow_src"], ops["pow_out"], ops["b_bar"], ops["c_out"],
      ops["lam_t"], ops["d_rows"])
    y_lat = outs[0].reshape(u_lat.shape)
    y_ctx = outs[1].reshape(u_ctx.shape) if ctx_out else None
    return y_lat, y_ctx


def _mix_ffn_kernel(*refs, n_chunks, add_pe):
    it = iter(refs)
    x_ref = next(it)
    pe_ref = next(it) if add_pe else None
    (y_ref, cat_ref, mod_ref, nw_mix_ref, nw_pre_ref, nw_post_ref, gluw_ref, glub_ref, wout_ref,
     wg_ref, wu_ref, wd_ref, o_ref, x1_ref, acc_ref) = (next(it) for _ in range(15))

    yt = jnp.concatenate([y_ref[:, r, :, :].reshape(B_WIDTH, CHUNK) for r in range(n_chunks)], axis=1)
    g = jax.nn.gelu(yt)
    gate = jax.nn.sigmoid(_dot(gluw_ref[...], g.astype(BF16)) + glub_ref[...])
    st = (g * gate).astype(BF16)
    at = jnp.concatenate([cat_ref[0, r, 0:A_WIDTH, :] for r in range(n_chunks)], axis=1)
    pt = jnp.concatenate([cat_ref[0, r, A_WIDTH:, :] for r in range(n_chunks)], axis=1)
    cat_t = jnp.concatenate([at, st, pt], axis=0)
    m = _dot_tn(cat_t, wout_ref[...])
    x = x_ref[0]
    if add_pe:
        x = x + pe_ref[...]
    x1 = x + mod_ref[2, 0] * _rms(m, nw_mix_ref[...])
    x1_ref[...] = x1

    h = (_rms(x1, nw_pre_ref[...]) * (1.0 + mod_ref[4, 0]) + mod_ref[3, 0]).astype(BF16)
    for k in range(N_FF_CHUNKS):
        act = (jax.nn.silu(_dot(h, wg_ref[k])) * _dot(h, wu_ref[k])).astype(BF16)
        part = _dot(act, wd_ref[k])
        if k == 0:
            acc_ref[...] = part
        else:
            acc_ref[...] += part
    o_ref[0] = x1_ref[...] + mod_ref[5, 0] * _rms(acc_ref[...], nw_post_ref[...])


def _mix_ffn_call(x, pe, y, cat, mod_all, layer, mod_row, prep, *, n_chunks):
    bsz, seq, _ = x.shape
    tt = n_chunks * CHUNK
    add_pe = pe is not None
    in_specs = [pl.BlockSpec((1, tt, D_MODEL), lambda b, c: (b, c, 0))]
    args = [x]
    if add_pe:
        in_specs.append(pl.BlockSpec((tt, D_MODEL), lambda b, c: (c, 0)))
        args.append(pe)
    in_specs += [
        pl.BlockSpec((SSM_GROUPS, n_chunks, SSM_GROUP, CHUNK), lambda b, c: (0, c, b, 0)),
        pl.BlockSpec((1, n_chunks, 2 * A_WIDTH, CHUNK), lambda b, c: (b, c, 0, 0)),
        _mod_spec(layer, mod_row),
        _const_spec((1, D_MODEL), layer),
        _const_spec((1, D_MODEL), layer),
        _const_spec((1, D_MODEL), layer),
        _const_spec((B_WIDTH, B_WIDTH), layer),
        _const_spec((B_WIDTH, 1), layer),
        _const_spec((D_MODEL, D_MODEL), layer),
        _const_spec((N_FF_CHUNKS, D_MODEL, FF_CHUNK), layer),
        _const_spec((N_FF_CHUNKS, D_MODEL, FF_CHUNK), layer),
        _const_spec((N_FF_CHUNKS, FF_CHUNK, D_MODEL), layer),
    ]
    args += [y, cat, mod_all, prep["nw_mix_post"], prep["nw_ffn_pre"], prep["nw_ffn_post"], prep["gluw_t"],
             prep["glub_col"], prep["wout"], prep["wg"], prep["wu"], prep["wd"]]
    return pl.pallas_call(
        functools.partial(_mix_ffn_kernel, n_chunks=n_chunks, add_pe=add_pe),
        grid=(bsz, seq // tt),
        in_specs=in_specs,
        out_specs=pl.BlockSpec((1, tt, D_MODEL), lambda b, c: (b, c, 0)),
        out_shape=jax.ShapeDtypeStruct(x.shape, F32),
        scratch_shapes=[pltpu.VMEM((tt, D_MODEL), F32), pltpu.VMEM((tt, D_MODEL), F32)],
        compiler_params=pltpu.CompilerParams(
            dimension_semantics=("parallel", "parallel"), vmem_limit_bytes=V7X_VMEM_LIMIT),
        name="mix_out_ffn",
    )(*args)


def _sincos_2d(rows, cols, dim):
    quarter = dim // 4
    omega = 1.0 / (10000.0 ** (jnp.arange(quarter, dtype=F32) / quarter))
    r = jnp.arange(rows, dtype=F32)[:, None] * omega
    cc = jnp.arange(cols, dtype=F32)[:, None] * omega
    er = jnp.concatenate([jnp.sin(r), jnp.cos(r)], axis=-1)
    ec = jnp.concatenate([jnp.sin(cc), jnp.cos(cc)], axis=-1)
    pe = jnp.concatenate([jnp.broadcast_to(er[:, None, :], (rows, cols, dim // 2)),
                          jnp.broadcast_to(ec[None, :, :], (rows, cols, dim // 2))], axis=-1)
    return pe.reshape(rows * cols, dim)


def _pool_bands(row_len):
    pos = np.arange(POOL_SEG)
    start = (pos // row_len) * row_len
    t = pos - start
    bands, icnts = [], []
    for w in POOL_WINDOWS:
        lo = np.clip(t - w // 2, 0, row_len) + start
        hi = np.clip(t - w // 2 + w, 0, row_len) + start
        bands.append(((pos[:, None] >= lo[None, :]) & (pos[:, None] < hi[None, :])).astype(np.float32))
        icnts.append((1.0 / (hi - lo).astype(np.float32))[None, :])
    return jnp.asarray(np.stack(bands), BF16), jnp.asarray(np.stack(icnts), F32)


def _ssm_operators(lam_re, lam_im, log_dt, b_re, b_im, c_re, c_im, d):
    t_len = CHUNK
    groups = lam_re.shape[1]
    dt = jnp.exp(log_dt)[..., None]
    a_re, a_im = lam_re * dt, lam_im * dt
    mag = jnp.exp(a_re)
    lb_re, lb_im = mag * jnp.cos(a_im), mag * jnp.sin(a_im)
    den = lam_re * lam_re + lam_im * lam_im
    q_re = ((lb_re - 1.0) * lam_re + lb_im * lam_im) / den
    q_im = (lb_im * lam_re - (lb_re - 1.0) * lam_im) / den
    bb_re = q_re[..., None] * b_re - q_im[..., None] * b_im
    bb_im = q_re[..., None] * b_im + q_im[..., None] * b_re
    k = jnp.arange(t_len + 1, dtype=F32)
    p_mag = jnp.exp(a_re[..., None] * k)
    p_re = p_mag * jnp.cos(a_im[..., None] * k)
    p_im = p_mag * jnp.sin(a_im[..., None] * k)

    cb_re = c_re[:, :, None] * jnp.swapaxes(bb_re, 2, 3)[:, :, :, None] \
        - c_im[:, :, None] * jnp.swapaxes(bb_im, 2, 3)[:, :, :, None]
    cb_im = c_re[:, :, None] * jnp.swapaxes(bb_im, 2, 3)[:, :, :, None] \
        + c_im[:, :, None] * jnp.swapaxes(bb_re, 2, 3)[:, :, :, None]
    hp = lax.Precision.HIGHEST
    resp = (jnp.einsum('dghop,dgpk->dghok', cb_re, p_re[..., :t_len], precision=hp)
            - jnp.einsum('dghop,dgpk->dghok', cb_im, p_im[..., :t_len], precision=hp))
    r_f, r_b = resp[0], resp[1]
    lag_table = jnp.concatenate([jnp.zeros_like(r_f[..., :1]), r_b[..., :0:-1],
                                 r_f[..., :1] + r_b[..., :1], r_f[..., 1:]], axis=-1)

    def both_dirs(fwd, bwd, axis):
        return jnp.concatenate([fwd, bwd], axis=axis)

    def re_im(re, im):
        return jnp.stack([re, im], axis=1)

    pow_src = re_im(jnp.swapaxes(both_dirs(p_re[0][..., t_len - 1::-1], p_re[1][..., :t_len], 1), 1, 2),
                    jnp.swapaxes(both_dirs(p_im[0][..., t_len - 1::-1], p_im[1][..., :t_len], 1), 1, 2))
    pow_out = re_im(both_dirs(p_re[0][..., 1:], p_re[1][..., :0:-1], 1),
                    both_dirs(p_im[0][..., 1:], p_im[1][..., :0:-1], 1))
    b_bar = re_im(jnp.swapaxes(both_dirs(bb_re[0], bb_re[1], 1), 1, 2),
                  jnp.swapaxes(both_dirs(bb_im[0], bb_im[1], 1), 1, 2))
    c_out = re_im(jnp.swapaxes(both_dirs(c_re[0], c_re[1], 2), 1, 2),
                  jnp.swapaxes(both_dirs(c_im[0], c_im[1], 2), 1, 2))
    lam_t = jnp.concatenate([p_re[0][..., t_len], p_re[1][..., t_len],
                             p_im[0][..., t_len], p_im[1][..., t_len]], axis=-1)[:, None, :]
    d_rows = jnp.broadcast_to(d[:, :, None], (groups, SSM_GROUP, CHUNK))
    return {"lag_table": lag_table, "pow_src": pow_src, "pow_out": pow_out, "b_bar": b_bar, "c_out": c_out,
            "lam_t": lam_t, "d_rows": d_rows}


def _fold_layers(p):
    p = jnp.swapaxes(p, 0, 1)
    return p.reshape((2, p.shape[1] * p.shape[2]) + p.shape[3:])


def _block_diag_t(blocks):
    depth, n, k, _ = blocks.shape
    eye = jnp.eye(n, dtype=blocks.dtype)
    out = jnp.einsum('dnab,nm->dnbma', blocks, eye)
    return out.reshape(depth, n * k, n * k)


def kernel(x, c, ctx, c_ctx, w_mod, b_mod, norm_mix_pre, norm_mix_post, norm_ffn_pre, norm_ffn_post, w_in, w_out, sgu_w, sgu_b, ssm_lam_re, ssm_lam_im, ssm_log_dt, ssm_b_re, ssm_b_im, ssm_c_re, ssm_c_im, ssm_d, glu_w, glu_b, pool_w, pool_scale, ffn_w_gate, ffn_w_up, ffn_w_down):
    bsz, n_lat, _ = x.shape
    n_ctx = ctx.shape[1]
    depth = w_mod.shape[0]
    assert bsz + 1 <= MOD_ROWS and n_lat % (8 * CHUNK) == 0 and n_ctx == POOL_SEG

    pe = _sincos_2d(n_lat // GRID_W, GRID_W, D_MODEL)
    c_all = jnp.concatenate([c, c_ctx[None, :], jnp.zeros((MOD_ROWS - bsz - 1, D_MODEL), F32)], axis=0)
    mod_all = _modulation(c_all, w_mod, b_mod)
    mod_all = jnp.transpose(mod_all.reshape(depth, MOD_ROWS, 6, 1, D_MODEL), (0, 2, 1, 3, 4))
    ctx_row = bsz

    band_lat = _pool_bands(GRID_W)
    band_ctx = _pool_bands(n_ctx)

    prep = {
        "nw_mix_pre": norm_mix_pre[:, None, :], "nw_mix_post": norm_mix_post[:, None, :],
        "nw_ffn_pre": norm_ffn_pre[:, None, :], "nw_ffn_post": norm_ffn_post[:, None, :],
        "win_t": jnp.swapaxes(w_in, 1, 2).astype(BF16),
        "sguw_t": jnp.swapaxes(sgu_w, 2, 3).astype(BF16),
        "sgub": sgu_b[:, :, None, :],
        "poolw_t": _block_diag_t(pool_w).astype(BF16),
        "pscale": pool_scale[:, :, None],
        "gluw_t": jnp.swapaxes(glu_w, 1, 2).astype(BF16),
        "glub_col": glu_b[:, :, None],
        "wout": w_out.astype(BF16),
        "wg": jnp.transpose(ffn_w_gate.reshape(depth, D_MODEL, N_FF_CHUNKS, FF_CHUNK), (0, 2, 1, 3)).astype(BF16),
        "wu": jnp.transpose(ffn_w_up.reshape(depth, D_MODEL, N_FF_CHUNKS, FF_CHUNK), (0, 2, 1, 3)).astype(BF16),
        "wd": ffn_w_down.reshape(depth, N_FF_CHUNKS, FF_CHUNK, D_MODEL).astype(BF16),
    }
    ops = _ssm_operators(_fold_layers(ssm_lam_re), _fold_layers(ssm_lam_im), _fold_layers(ssm_log_dt),
                         _fold_layers(ssm_b_re), _fold_layers(ssm_b_im), _fold_layers(ssm_c_re),
                         _fold_layers(ssm_c_im), ssm_d.reshape(depth * SSM_GROUPS, SSM_GROUP))

    x_lat, x_ctx = x, ctx
    for i in range(depth):
        need_ctx = i < depth - 1
        pe_i = pe if i == 0 else None
        u_lat, cat_lat = _proj_call(x_lat, pe_i, mod_all, i, None, prep, band_lat, n_chunks=8, ssm_only=False)
        if need_ctx:
            u_ctx, cat_ctx = _proj_call(x_ctx, None, mod_all, i, ctx_row, prep, band_ctx, n_chunks=2, ssm_only=False)
        else:
            u_ctx = _proj_call(x_ctx, None, mod_all, i, ctx_row, prep, None, n_chunks=2, ssm_only=True)
        y_lat, y_ctx = _ssm_call(u_lat, u_ctx, ops, i, ctx_out=need_ctx)
        x_lat = _mix_ffn_call(x_lat, pe_i, y_lat, cat_lat, mod_all, i, None, prep, n_chunks=4)
        if need_ctx:
            x_ctx = _mix_ffn_call(x_ctx, None, y_ctx, cat_ctx, mod_all, i, ctx_row, prep, n_chunks=2)
    return x_lat
```

```python
import functools

import numpy as np
import jax
import jax.numpy as jnp
from jax import lax
from jax.experimental import pallas as pl
from jax.experimental.pallas import tpu as pltpu

D_MODEL = 1024
GRID_W = 64
EPS = 1e-6
A_WIDTH = 256
A_HEADS = 4
A_HEAD_DIM = 64
CHUNK = 128
B_WIDTH = 512
SSM_GROUP = 16
SSM_GROUPS = 32
SSM_STATE = 64
C_WIDTH = 256
POOL_WINDOWS = (2, 4, 8, 16)
POOL_GROUP = 64
D_IN = 1280
D_FF = 2816
S_LO = 2 * A_WIDTH
P_LO = S_LO + B_WIDTH
POOL_SEG = 256
FF_CHUNK = 256
N_FF_CHUNKS = D_FF // FF_CHUNK
SSM_WIDE = SSM_GROUP * CHUNK
MOD_ROWS = 40
V7X_VMEM_LIMIT = 56 * 1024 * 1024

F32 = jnp.float32
BF16 = jnp.bfloat16


def _rms(x, w):
    return x * lax.rsqrt(jnp.mean(x * x, axis=-1, keepdims=True) + EPS) * w


def _dot(a, b):
    return jnp.dot(a, b, preferred_element_type=F32)


def _dot_nt(a, b):
    return lax.dot_general(a, b, (((1,), (1,)), ((), ())), preferred_element_type=F32)


def _dot_tn(a, b):
    return lax.dot_general(a, b, (((0,), (0,)), ((), ())), preferred_element_type=F32)


def _const_spec(shape, layer=None):
    if layer is None:
        return pl.BlockSpec(shape, lambda *_: (0,) * len(shape), pipeline_mode=pl.Buffered(1))
    return pl.BlockSpec((None,) + shape, lambda *_: (layer,) + (0,) * len(shape), pipeline_mode=pl.Buffered(1))


def _mod_spec(layer, mod_row):
    row = (lambda b: b) if mod_row is None else (lambda b: mod_row)
    return pl.BlockSpec((None, 6, 1, 1, D_MODEL), lambda b, c: (layer, 0, row(b), 0, 0))


def _mod_kernel(c_ref, w_ref, b_ref, o_ref):
    s = jax.nn.silu(c_ref[...])
    o_ref[0] = jnp.dot(s, w_ref[0], preferred_element_type=F32,
                       precision=lax.Precision.HIGHEST) + b_ref[0]


def _modulation(c_all, w_mod, b_mod):
    depth = w_mod.shape[0]
    n_tiles = 6
    return pl.pallas_call(
        _mod_kernel,
        grid=(depth, n_tiles),
        in_specs=[
            pl.BlockSpec((MOD_ROWS, D_MODEL), lambda i, j: (0, 0)),
            pl.BlockSpec((1, D_MODEL, D_MODEL), lambda i, j: (i, 0, j)),
            pl.BlockSpec((1, 1, D_MODEL), lambda i, j: (i, 0, j)),
        ],
        out_specs=pl.BlockSpec((1, MOD_ROWS, D_MODEL), lambda i, j: (i, 0, j)),
        out_shape=jax.ShapeDtypeStruct((depth, MOD_ROWS, 6 * D_MODEL), F32),
        name="modulation",
    )(c_all, w_mod, b_mod.reshape(depth, 1, 6 * D_MODEL))


def _proj_kernel(*refs, n_chunks, add_pe, ssm_only):
    it = iter(refs)
    x_ref = next(it)
    pe_ref = next(it) if add_pe else None
    mod_ref, nw_ref, win_ref = next(it), next(it), next(it)
    if not ssm_only:
        sguw_ref, sgub_ref, band_ref, icnt_ref, poolw_ref, pscale_ref = (next(it) for _ in range(6))
    u_ref = next(it)
    cat_ref = None if ssm_only else next(it)

    tt = n_chunks * CHUNK
    x = x_ref[0]
    if add_pe:
        x = x + pe_ref[...]
    h = _rms(x, nw_ref[...] * (1.0 + mod_ref[1, 0])) + mod_ref[0, 0]
    hb = h.astype(BF16)

    zs = _dot_nt(win_ref[S_LO:P_LO, :], hb)
    for r in range(n_chunks):
        u_ref[:, r, :, :] = zs[:, r * CHUNK:(r + 1) * CHUNK].reshape(SSM_GROUPS, SSM_GROUP, CHUNK)
    if ssm_only:
        return

    za = jax.nn.gelu(_dot_nt(win_ref[0:S_LO, :], hb))
    v = za[A_WIDTH:, :].reshape(A_HEADS, A_HEAD_DIM, tt)
    mu = jnp.mean(v, axis=1, keepdims=True)
    vc = v - mu
    var = jnp.mean(vc * vc, axis=1, keepdims=True)
    vn = (vc * lax.rsqrt(var + EPS)).astype(BF16)
    for hd in range(A_HEADS):
        lhs = jnp.concatenate([vn[hd][:, r * CHUNK:(r + 1) * CHUNK] for r in range(n_chunks)], axis=0)
        s = _dot(lhs, sguw_ref[hd]) + sgub_ref[hd]
        for r in range(n_chunks):
            ug = za[hd * A_HEAD_DIM:(hd + 1) * A_HEAD_DIM, r * CHUNK:(r + 1) * CHUNK]
            a = ug * s[r * A_HEAD_DIM:(r + 1) * A_HEAD_DIM, :]
            cat_ref[0, r, hd * A_HEAD_DIM:(hd + 1) * A_HEAD_DIM, :] = a.astype(BF16)

    zp = _dot_nt(win_ref[P_LO:D_IN, :], hb)
    n_seg = tt // POOL_SEG
    diffs = []
    for i in range(len(POOL_WINDOWS)):
        pg = zp[i * POOL_GROUP:(i + 1) * POOL_GROUP, :]
        lhs = jnp.concatenate([pg[:, j * POOL_SEG:(j + 1) * POOL_SEG] for j in range(n_seg)], axis=0)
        m = _dot(lhs.astype(BF16), band_ref[i]) * icnt_ref[i]
        m = jnp.concatenate([m[j * POOL_GROUP:(j + 1) * POOL_GROUP, :] for j in range(n_seg)], axis=1)
        diffs.append(m - pg)
    dt = jnp.concatenate(diffs, axis=0).astype(BF16)
    pool = _dot(poolw_ref[...], dt) * pscale_ref[...]
    for r in range(n_chunks):
        cat_ref[0, r, A_WIDTH:, :] = pool[:, r * CHUNK:(r + 1) * CHUNK].astype(BF16)


def _proj_call(x, pe, mod_all, layer, mod_row, prep, pool_tables, *, n_chunks, ssm_only):
    bsz, seq, _ = x.shape
    tt = n_chunks * CHUNK
    chunks = seq // CHUNK
    add_pe = pe is not None

    in_specs = [pl.BlockSpec((1, tt, D_MODEL), lambda b, c: (b, c, 0))]
    args = [x]
    if add_pe:
        in_specs.append(pl.BlockSpec((tt, D_MODEL), lambda b, c: (c, 0)))
        args.append(pe)
    in_specs += [_mod_spec(layer, mod_row), _const_spec((1, D_MODEL), layer), _const_spec((D_IN, D_MODEL), layer)]
    args += [mod_all, prep["nw_mix_pre"], prep["win_t"]]
    if not ssm_only:
        band, icnt = pool_tables
        in_specs += [
            _const_spec((A_HEADS, CHUNK, CHUNK), layer),
            _const_spec((A_HEADS, 1, CHUNK), layer),
            _const_spec((len(POOL_WINDOWS), POOL_SEG, POOL_SEG)),
            _const_spec((len(POOL_WINDOWS), 1, POOL_SEG)),
            _const_spec((C_WIDTH, C_WIDTH), layer),
            _const_spec((C_WIDTH, 1), layer),
        ]
        args += [prep["sguw_t"], prep["sgub"], band, icnt, prep["poolw_t"], prep["pscale"]]

    u_shape = jax.ShapeDtypeStruct((SSM_GROUPS, chunks, bsz * SSM_GROUP, CHUNK), F32)
    u_spec = pl.BlockSpec((SSM_GROUPS, n_chunks, SSM_GROUP, CHUNK), lambda b, c: (0, c, b, 0))
    if ssm_only:
        out_shape, out_specs = u_shape, u_spec
    else:
        out_shape = (u_shape, jax.ShapeDtypeStruct((bsz, chunks, 2 * A_WIDTH, CHUNK), BF16))
        out_specs = (u_spec, pl.BlockSpec((1, n_chunks, 2 * A_WIDTH, CHUNK), lambda b, c: (b, c, 0, 0)))

    return pl.pallas_call(
        functools.partial(_proj_kernel, n_chunks=n_chunks, add_pe=add_pe, ssm_only=ssm_only),
        grid=(bsz, seq // tt),
        in_specs=in_specs,
        out_specs=out_specs,
        out_shape=out_shape,
        compiler_params=pltpu.CompilerParams(
            dimension_semantics=("parallel", "parallel"), vmem_limit_bytes=V7X_VMEM_LIMIT),
        name="proj_ssm_only" if ssm_only else "proj_local_mix",
    )(*args)


def _toeplitz_block(lag_ref, k_ref, hh, oo, rows):
    table = jnp.broadcast_to(lag_ref[0, hh, oo:oo + 1, :], (CHUNK, 2 * CHUNK))
    shifted = pltpu.roll(table, 0, 1, stride=1, stride_axis=0)
    k_ref[rows, oo * CHUNK:(oo + 1) * CHUNK] = shifted[:, CHUNK:].astype(BF16)


def _ssm_group(ul_ref, uc_ref, lag_next_ref, pe_ref, pm_ref, bb_ref, cc_ref, lam_ref, d_ref, yl_ref, yc_ref,
               k_now_ref, k_next_ref, e_ref, min_ref, lhs_ref, sin_ref, *, rows_lat, rows_ctx, bsz):
    n_lat = rows_lat // bsz
    n_ctx = rows_ctx // bsz
    half = 2 * SSM_STATE

    p_re, p_im = pe_ref[0, 0], pe_ref[0, 1]
    q_re, q_im = pm_ref[0, 0], pm_ref[0, 1]
    for hh in range(SSM_GROUP):
        rows = slice(hh * CHUNK, (hh + 1) * CHUNK)
        b_re, b_im = bb_ref[0, 0, hh:hh + 1, :], bb_ref[0, 1, hh:hh + 1, :]
        e_ref[rows, 0:half] = (b_re * p_re - b_im * p_im).astype(BF16)
        e_ref[rows, half:] = (b_re * p_im + b_im * p_re).astype(BF16)
        c_re, c_im = cc_ref[0, 0, :, hh:hh + 1], cc_ref[0, 1, :, hh:hh + 1]
        min_ref[0:half, rows] = (c_re * q_re - c_im * q_im).astype(BF16)
        min_ref[half:, rows] = (-(c_re * q_im + c_im * q_re)).astype(BF16)

    for hh in range(SSM_GROUP):
        lanes = slice(hh * CHUNK, (hh + 1) * CHUNK)
        lhs_ref[0:rows_lat, lanes] = ul_ref[0, pl.ds(hh, rows_lat, stride=SSM_GROUP), :].astype(BF16)
        lhs_ref[rows_lat:, lanes] = uc_ref[0, pl.ds(hh, rows_ctx, stride=SSM_GROUP), :].astype(BF16)
    sloc = _dot(lhs_ref[...], e_ref[...])
    lam_re = lam_ref[0, :, 0:half]
    lam_im = lam_ref[0, :, half:]
    fwd_lane = lax.broadcasted_iota(jnp.int32, (bsz, half), 1) < SSM_STATE

    def step(state, add):
        s_re, s_im = state
        a_re, a_im = add
        return (lam_re * s_re - lam_im * s_im + a_re, lam_re * s_im + lam_im * s_re + a_im)

    def rows_of(base, c):
        blk = sloc[base + c * bsz:base + (c + 1) * bsz, :]
        return blk[:, 0:half], blk[:, half:]

    def put(base, c, f_state, b_state):
        sin_ref[base + c * bsz:base + (c + 1) * bsz, 0:half] = jnp.where(
            fwd_lane, f_state[0], b_state[0]).astype(BF16)
        sin_ref[base + c * bsz:base + (c + 1) * bsz, half:] = jnp.where(
            fwd_lane, f_state[1], b_state[1]).astype(BF16)

    def sweep(base, count, f0, b0):
        f_in, b_in = [None] * count, [None] * count
        f, b = f0, b0
        for c in range(count):
            f_in[c] = f
            f = step(f, rows_of(base, c))
        f_end = f
        for c in reversed(range(count)):
            b_in[c] = b
            b = step(b, rows_of(base, c))
        for c in range(count):
            put(base, c, f_in[c], b_in[c])
        return f_end, b

    zero = (jnp.zeros((bsz, half), F32), jnp.zeros((bsz, half), F32))
    f_end, b_end = sweep(rows_lat, n_ctx, zero, zero)
    sweep(0, n_lat, f_end, b_end)

    pair = 2 * CHUNK
    for q in range(SSM_WIDE // pair):
        cols = slice(q * pair, (q + 1) * pair)
        y = _dot(lhs_ref[...], k_now_ref[:, cols]) + _dot(sin_ref[...], min_ref[:, cols])
        for k in range(2):
            hh = 2 * q + k
            piece = y[:, k * CHUNK:(k + 1) * CHUNK]
            skip = d_ref[0, hh:hh + 1, :]
            lat_rows = pl.ds(hh, rows_lat, stride=SSM_GROUP)
            yl_ref[0, lat_rows, :] = piece[0:rows_lat] + skip * ul_ref[0, lat_rows, :]
            if yc_ref is not None:
                ctx_rows = pl.ds(hh, rows_ctx, stride=SSM_GROUP)
                yc_ref[0, ctx_rows, :] = piece[rows_lat:] + skip * uc_ref[0, ctx_rows, :]
            for oo in range(SSM_GROUP):
                _toeplitz_block(lag_next_ref, k_next_ref, hh, oo, slice(hh * CHUNK, (hh + 1) * CHUNK))


def _ssm_kernel(*refs, rows_lat, rows_ctx, bsz, ctx_out):
    it = iter(refs)
    ul_ref, uc_ref, lag_now_ref, lag_next_ref, pe_ref, pm_ref, bb_ref, cc_ref, lam_ref, d_ref, yl_ref = (
        next(it) for _ in range(11))
    yc_ref = next(it) if ctx_out else None
    k_even_ref, k_odd_ref, e_ref, min_ref, lhs_ref, sin_ref = (next(it) for _ in range(6))
    g = pl.program_id(0)

    @pl.when(g == 0)
    def _():
        def per_in_channel(hh, carry):
            rows = pl.ds(pl.multiple_of(hh * CHUNK, CHUNK), CHUNK)
            for oo in range(SSM_GROUP):
                _toeplitz_block(lag_now_ref, k_even_ref, hh, oo, rows)
            return carry

        lax.fori_loop(0, SSM_GROUP, per_in_channel, 0)

    group = functools.partial(_ssm_group, ul_ref, uc_ref, lag_next_ref, pe_ref, pm_ref, bb_ref, cc_ref, lam_ref,
                              d_ref, yl_ref, yc_ref, e_ref=e_ref, min_ref=min_ref, lhs_ref=lhs_ref,
                              sin_ref=sin_ref, rows_lat=rows_lat, rows_ctx=rows_ctx, bsz=bsz)

    @pl.when(g % 2 == 0)
    def _():
        group(k_now_ref=k_even_ref, k_next_ref=k_odd_ref)

    @pl.when(g % 2 == 1)
    def _():
        group(k_now_ref=k_odd_ref, k_next_ref=k_even_ref)


def _ssm_call(u_lat, u_ctx, ops, layer, *, ctx_out):
    groups, c_lat = u_lat.shape[:2]
    c_ctx = u_ctx.shape[1]
    bsz = u_lat.shape[2] // SSM_GROUP
    rows_lat, rows_ctx = c_lat * bsz, c_ctx * bsz
    rows = rows_lat + rows_ctx
    off = layer * groups
    state_w = 4 * SSM_STATE
    ul = u_lat.reshape(groups, rows_lat * SSM_GROUP, CHUNK)
    uc = u_ctx.reshape(groups, rows_ctx * SSM_GROUP, CHUNK)
    per_g = lambda g: (g, 0, 0)
    per_lg3 = lambda g: (g + off, 0, 0)
    per_lg4 = lambda g: (g + off, 0, 0, 0)
    next_lg4 = lambda g: (jnp.minimum(g + 1, groups - 1) + off, 0, 0, 0)
    out_shape = [jax.ShapeDtypeStruct(ul.shape, F32)]
    out_specs = [pl.BlockSpec((1, rows_lat * SSM_GROUP, CHUNK), per_g)]
    if ctx_out:
        out_shape.append(jax.ShapeDtypeStruct(uc.shape, F32))
        out_specs.append(pl.BlockSpec((1, rows_ctx * SSM_GROUP, CHUNK), per_g))
    outs = pl.pallas_call(
        functools.partial(_ssm_kernel, rows_lat=rows_lat, rows_ctx=rows_ctx, bsz=bsz, ctx_out=ctx_out),
        grid=(groups,),
        in_specs=[
            pl.BlockSpec((1, rows_lat * SSM_GROUP, CHUNK), per_g),
            pl.BlockSpec((1, rows_ctx * SSM_GROUP, CHUNK), per_g),
            pl.BlockSpec((1, SSM_GROUP, SSM_GROUP, 2 * CHUNK), per_lg4),
            pl.BlockSpec((1, SSM_GROUP, SSM_GROUP, 2 * CHUNK), next_lg4),
            pl.BlockSpec((1, 2, CHUNK, 2 * SSM_STATE), per_lg4),
            pl.BlockSpec((1, 2, 2 * SSM_STATE, CHUNK), per_lg4),
            pl.BlockSpec((1, 2, SSM_GROUP, 2 * SSM_STATE), per_lg4),
            pl.BlockSpec((1, 2, 2 * SSM_STATE, SSM_GROUP), per_lg4),
            pl.BlockSpec((1, 1, state_w), per_lg3),
            pl.BlockSpec((1, SSM_GROUP, CHUNK), per_lg3),
        ],
        out_specs=out_specs,
        out_shape=out_shape,
        scratch_shapes=[
            pltpu.VMEM((SSM_WIDE, SSM_WIDE), BF16), pltpu.VMEM((SSM_WIDE, SSM_WIDE), BF16),
            pltpu.VMEM((SSM_WIDE, state_w), BF16), pltpu.VMEM((state_w, SSM_WIDE), BF16),
            pltpu.VMEM((rows, SSM_WIDE), BF16), pltpu.VMEM((rows, state_w), BF16),
        ],
        compiler_params=pltpu.CompilerParams(
            dimension_semantics=("arbitrary",), vmem_limit_bytes=V7X_VMEM_LIMIT),
        name="ssm_chunked_scan",
    )(ul, uc, ops["lag_table"], ops["lag_table"], ops["pow_src"], ops["pow_out"], ops["b_bar"], ops["c_out"],
      ops["lam_t"], ops["d_rows"])
    y_lat = outs[0].reshape(u_lat.shape)
    y_ctx = outs[1].reshape(u_ctx.shape) if ctx_out else None
    return y_lat, y_ctx


def _mix_ffn_kernel(*refs, n_chunks, add_pe):
    it = iter(refs)
    x_ref = next(it)
    pe_ref = next(it) if add_pe else None
    (y_ref, cat_ref, mod_ref, nw_mix_ref, nw_pre_ref, nw_post_ref, gluw_ref, glub_ref, wout_ref,
     wg_ref, wu_ref, wd_ref, o_ref, x1_ref, acc_ref) = (next(it) for _ in range(15))

    yt = jnp.concatenate([y_ref[:, r, :, :].reshape(B_WIDTH, CHUNK) for r in range(n_chunks)], axis=1)
    g = jax.nn.gelu(yt)
    gate = jax.nn.sigmoid(_dot(gluw_ref[...], g.astype(BF16)) + glub_ref[...])
    st = (g * gate).astype(BF16)
    at = jnp.concatenate([cat_ref[0, r, 0:A_WIDTH, :] for r in range(n_chunks)], axis=1)
    pt = jnp.concatenate([cat_ref[0, r, A_WIDTH:, :] for r in range(n_chunks)], axis=1)
    cat_t = jnp.concatenate([at, st, pt], axis=0)
    m = _dot_tn(cat_t, wout_ref[...])
    x = x_ref[0]
    if add_pe:
        x = x + pe_ref[...]
    x1 = x + _rms(m, nw_mix_ref[...] * mod_ref[2, 0])
    x1_ref[...] = x1

    h = (_rms(x1, nw_pre_ref[...] * (1.0 + mod_ref[4, 0])) + mod_ref[3, 0]).astype(BF16)
    for k in range(N_FF_CHUNKS):
        cols = slice(k * FF_CHUNK, (k + 1) * FF_CHUNK)
        act = (jax.nn.silu(_dot(h, wg_ref[:, cols])) * _dot(h, wu_ref[:, cols])).astype(BF16)
        part = _dot(act, wd_ref[cols, :])
        if k == 0:
            acc_ref[...] = part
        else:
            acc_ref[...] += part
    o_ref[0] = x1_ref[...] + _rms(acc_ref[...], nw_post_ref[...] * mod_ref[5, 0])


def _mix_ffn_call(x, pe, y, cat, mod_all, layer, mod_row, prep, *, n_chunks):
    bsz, seq, _ = x.shape
    tt = n_chunks * CHUNK
    add_pe = pe is not None
    in_specs = [pl.BlockSpec((1, tt, D_MODEL), lambda b, c: (b, c, 0))]
    args = [x]
    if add_pe:
        in_specs.append(pl.BlockSpec((tt, D_MODEL), lambda b, c: (c, 0)))
        args.append(pe)
    in_specs += [
        pl.BlockSpec((SSM_GROUPS, n_chunks, SSM_GROUP, CHUNK), lambda b, c: (0, c, b, 0)),
        pl.BlockSpec((1, n_chunks, 2 * A_WIDTH, CHUNK), lambda b, c: (b, c, 0, 0)),
        _mod_spec(layer, mod_row),
        _const_spec((1, D_MODEL), layer),
        _const_spec((1, D_MODEL), layer),
        _const_spec((1, D_MODEL), layer),
        _const_spec((B_WIDTH, B_WIDTH), layer),
        _const_spec((B_WIDTH, 1), layer),
        _const_spec((D_MODEL, D_MODEL), layer),
        _const_spec((D_MODEL, D_FF), layer),
        _const_spec((D_MODEL, D_FF), layer),
        _const_spec((D_FF, D_MODEL), layer),
    ]
    args += [y, cat, mod_all, prep["nw_mix_post"], prep["nw_ffn_pre"], prep["nw_ffn_post"], prep["gluw_t"],
             prep["glub_col"], prep["wout"], prep["wg"], prep["wu"], prep["wd"]]
    return pl.pallas_call(
        functools.partial(_mix_ffn_kernel, n_chunks=n_chunks, add_pe=add_pe),
        grid=(bsz, seq // tt),
        in_specs=in_specs,
        out_specs=pl.BlockSpec((1, tt, D_MODEL), lambda b, c: (b, c, 0)),
        out_shape=jax.ShapeDtypeStruct(x.shape, F32),
        scratch_shapes=[pltpu.VMEM((tt, D_MODEL), F32), pltpu.VMEM((tt, D_MODEL), F32)],
        compiler_params=pltpu.CompilerParams(
            dimension_semantics=("parallel", "parallel"), vmem_limit_bytes=V7X_VMEM_LIMIT),
        name="mix_out_ffn",
    )(*args)


def _sincos_2d(rows, cols, dim):
    quarter = dim // 4
    f32 = np.float32
    omega = (f32(1.0) / (f32(10000.0) ** (np.arange(quarter, dtype=f32) / f32(quarter)))).astype(f32)
    r = (np.arange(rows, dtype=f32)[:, None] * omega).astype(np.float64)
    cc = (np.arange(cols, dtype=f32)[:, None] * omega).astype(np.float64)
    er = np.concatenate([np.sin(r), np.cos(r)], axis=-1)
    ec = np.concatenate([np.sin(cc), np.cos(cc)], axis=-1)
    pe = np.concatenate([np.broadcast_to(er[:, None, :], (rows, cols, dim // 2)),
                         np.broadcast_to(ec[None, :, :], (rows, cols, dim // 2))], axis=-1)
    return jnp.asarray(pe.reshape(rows * cols, dim), F32)


def _pool_bands(row_len):
    pos = np.arange(POOL_SEG)
    start = (pos // row_len) * row_len
    t = pos - start
    bands, icnts = [], []
    for w in POOL_WINDOWS:
        lo = np.clip(t - w // 2, 0, row_len) + start
        hi = np.clip(t - w // 2 + w, 0, row_len) + start
        bands.append(((pos[:, None] >= lo[None, :]) & (pos[:, None] < hi[None, :])).astype(np.float32))
        icnts.append((1.0 / (hi - lo).astype(np.float32))[None, :])
    return jnp.asarray(np.stack(bands), BF16), jnp.asarray(np.stack(icnts), F32)


def _ssm_operators(lam_re, lam_im, log_dt, b_re, b_im, c_re, c_im, d):
    t_len = CHUNK
    groups = lam_re.shape[1]
    dt = jnp.exp(log_dt)[..., None]
    a_re, a_im = lam_re * dt, lam_im * dt
    mag = jnp.exp(a_re)
    lb_re, lb_im = mag * jnp.cos(a_im), mag * jnp.sin(a_im)
    den = lam_re * lam_re + lam_im * lam_im
    q_re = ((lb_re - 1.0) * lam_re + lb_im * lam_im) / den
    q_im = (lb_im * lam_re - (lb_re - 1.0) * lam_im) / den
    bb_re = q_re[..., None] * b_re - q_im[..., None] * b_im
    bb_im = q_re[..., None] * b_im + q_im[..., None] * b_re
    k = jnp.arange(t_len + 1, dtype=F32)
    p_mag = jnp.exp(a_re[..., None] * k)
    p_re = p_mag * jnp.cos(a_im[..., None] * k)
    p_im = p_mag * jnp.sin(a_im[..., None] * k)

    cb_re = c_re[:, :, None] * jnp.swapaxes(bb_re, 2, 3)[:, :, :, None] \
        - c_im[:, :, None] * jnp.swapaxes(bb_im, 2, 3)[:, :, :, None]
    cb_im = c_re[:, :, None] * jnp.swapaxes(bb_im, 2, 3)[:, :, :, None] \
        + c_im[:, :, None] * jnp.swapaxes(bb_re, 2, 3)[:, :, :, None]
    hp = lax.Precision.HIGHEST
    resp = (jnp.einsum('dghop,dgpk->dghok', cb_re, p_re[..., :t_len], precision=hp)
            - jnp.einsum('dghop,dgpk->dghok', cb_im, p_im[..., :t_len], precision=hp))
    r_f, r_b = resp[0], resp[1]
    lag_table = jnp.concatenate([jnp.zeros_like(r_f[..., :1]), r_b[..., :0:-1],
                                 r_f[..., :1] + r_b[..., :1], r_f[..., 1:]], axis=-1)

    def both_dirs(fwd, bwd, axis):
        return jnp.concatenate([fwd, bwd], axis=axis)

    def re_im(re, im):
        return jnp.stack([re, im], axis=1)

    pow_src = re_im(jnp.swapaxes(both_dirs(p_re[0][..., t_len - 1::-1], p_re[1][..., :t_len], 1), 1, 2),
                    jnp.swapaxes(both_dirs(p_im[0][..., t_len - 1::-1], p_im[1][..., :t_len], 1), 1, 2))
    pow_out = re_im(both_dirs(p_re[0][..., 1:], p_re[1][..., :0:-1], 1),
                    both_dirs(p_im[0][..., 1:], p_im[1][..., :0:-1], 1))
    b_bar = re_im(jnp.swapaxes(both_dirs(bb_re[0], bb_re[1], 1), 1, 2),
                  jnp.swapaxes(both_dirs(bb_im[0], bb_im[1], 1), 1, 2))
    c_out = re_im(jnp.swapaxes(both_dirs(c_re[0], c_re[1], 2), 1, 2),
                  jnp.swapaxes(both_dirs(c_im[0], c_im[1], 2), 1, 2))
    lam_t = jnp.concatenate([p_re[0][..., t_len], p_re[1][..., t_len],
                             p_im[0][..., t_len], p_im[1][..., t_len]], axis=-1)[:, None, :]
    d_rows = jnp.broadcast_to(d[:, :, None], (groups, SSM_GROUP, CHUNK))
    return {"lag_table": lag_table, "pow_src": pow_src, "pow_out": pow_out, "b_bar": b_bar, "c_out": c_out,
            "lam_t": lam_t, "d_rows": d_rows}


def _fold_layers(p):
    p = jnp.swapaxes(p, 0, 1)
    return p.reshape((2, p.shape[1] * p.shape[2]) + p.shape[3:])


def _block_diag_t(blocks):
    depth, n, k, _ = blocks.shape
    eye = jnp.eye(n, dtype=blocks.dtype)
    out = jnp.einsum('dnab,nm->dnbma', blocks, eye)
    return out.reshape(depth, n * k, n * k)


def kernel(x, c, ctx, c_ctx, w_mod, b_mod, norm_mix_pre, norm_mix_post, norm_ffn_pre, norm_ffn_post, w_in, w_out, sgu_w, sgu_b, ssm_lam_re, ssm_lam_im, ssm_log_dt, ssm_b_re, ssm_b_im, ssm_c_re, ssm_c_im, ssm_d, glu_w, glu_b, pool_w, pool_scale, ffn_w_gate, ffn_w_up, ffn_w_down):
    bsz, n_lat, _ = x.shape
    n_ctx = ctx.shape[1]
    depth = w_mod.shape[0]
    assert bsz + 1 <= MOD_ROWS and n_lat % (8 * CHUNK) == 0 and n_ctx == POOL_SEG

    pe = _sincos_2d(n_lat // GRID_W, GRID_W, D_MODEL)
    c_all = jnp.concatenate([c, c_ctx[None, :], jnp.zeros((MOD_ROWS - bsz - 1, D_MODEL), F32)], axis=0)
    mod_all = _modulation(c_all, w_mod, b_mod)
    mod_all = jnp.transpose(mod_all.reshape(depth, MOD_ROWS, 6, 1, D_MODEL), (0, 2, 1, 3, 4))
    ctx_row = bsz

    band_lat = _pool_bands(GRID_W)
    band_ctx = _pool_bands(n_ctx)

    prep = {
        "nw_mix_pre": norm_mix_pre[:, None, :], "nw_mix_post": norm_mix_post[:, None, :],
        "nw_ffn_pre": norm_ffn_pre[:, None, :], "nw_ffn_post": norm_ffn_post[:, None, :],
        "win_t": jnp.swapaxes(w_in, 1, 2).astype(BF16),
        "sguw_t": jnp.swapaxes(sgu_w, 2, 3).astype(BF16),
        "sgub": sgu_b[:, :, None, :],
        "poolw_t": _block_diag_t(pool_w).astype(BF16),
        "pscale": pool_scale[:, :, None],
        "gluw_t": jnp.swapaxes(glu_w, 1, 2).astype(BF16),
        "glub_col": glu_b[:, :, None],
        "wout": w_out.astype(BF16),
        "wg": ffn_w_gate.astype(BF16),
        "wu": ffn_w_up.astype(BF16),
        "wd": ffn_w_down.astype(BF16),
    }
    ops = _ssm_operators(_fold_layers(ssm_lam_re), _fold_layers(ssm_lam_im), _fold_layers(ssm_log_dt),
                         _fold_layers(ssm_b_re), _fold_layers(ssm_b_im), _fold_layers(ssm_c_re),
                         _fold_layers(ssm_c_im), ssm_d.reshape(depth * SSM_GROUPS, SSM_GROUP))

    x_lat, x_ctx = x, ctx
    for i in range(depth):
        need_ctx = i < depth - 1
        pe_i = pe if i == 0 else None
        u_lat, cat_lat = _proj_call(x_lat, pe_i, mod_all, i, None, prep, band_lat, n_chunks=8, ssm_only=False)
        if need_ctx:
            u_ctx, cat_ctx = _proj_call(x_ctx, None, mod_all, i, ctx_row, prep, band_ctx, n_chunks=2, ssm_only=False)
        else:
            u_ctx = _proj_call(x_ctx, None, mod_all, i, ctx_row, prep, None, n_chunks=2, ssm_only=True)
        y_lat, y_ctx = _ssm_call(u_lat, u_ctx, ops, i, ctx_out=need_ctx)
        x_lat = _mix_ffn_call(x_lat, pe_i, y_lat, cat_lat, mod_all, i, None, prep, n_chunks=4)
        if need_ctx:
            x_ctx = _mix_ffn_call(x_ctx, None, y_ctx, cat_ctx, mod_all, i, ctx_row, prep, n_chunks=2)
    return x_lat
```

```python
import functools

import numpy as np
import jax
import jax.numpy as jnp
from jax import lax
from jax.experimental import pallas as pl
from jax.experimental.pallas import tpu as pltpu

D_MODEL = 1024
GRID_W = 64
EPS = 1e-6
A_WIDTH = 256
A_HEADS = 4
A_HEAD_DIM = 64
CHUNK = 128
B_WIDTH = 512
SSM_GROUP = 16
SSM_GROUPS = 32
SSM_STATE = 64
C_WIDTH = 256
POOL_WINDOWS = (2, 4, 8, 16)
POOL_GROUP = 64
D_IN = 1280
D_FF = 2816
S_LO = 2 * A_WIDTH
P_LO = S_LO + B_WIDTH
POOL_SEG = 256
FF_CHUNK = 256
N_FF_CHUNKS = D_FF // FF_CHUNK
SSM_WIDE = SSM_GROUP * CHUNK
MOD_ROWS = 40
V7X_VMEM_LIMIT = 56 * 1024 * 1024

F32 = jnp.float32
BF16 = jnp.bfloat16


def _rms(x, w):
    return x * lax.rsqrt(jnp.mean(x * x, axis=-1, keepdims=True) + EPS) * w


def _dot(a, b):
    return jnp.dot(a, b, preferred_element_type=F32)


def _dot_nt(a, b):
    return lax.dot_general(a, b, (((1,), (1,)), ((), ())), preferred_element_type=F32)


def _dot_tn(a, b):
    return lax.dot_general(a, b, (((0,), (0,)), ((), ())), preferred_element_type=F32)


def _const_spec(shape, layer=None):
    if layer is None:
        return pl.BlockSpec(shape, lambda *_: (0,) * len(shape), pipeline_mode=pl.Buffered(1))
    return pl.BlockSpec((None,) + shape, lambda *_: (layer,) + (0,) * len(shape), pipeline_mode=pl.Buffered(1))


def _mod_spec(layer, mod_row):
    row = (lambda b: b) if mod_row is None else (lambda b: mod_row)
    return pl.BlockSpec((None, 6, 1, 1, D_MODEL), lambda b, c: (layer, 0, row(b), 0, 0))


def _mod_kernel(c_ref, w_ref, b_ref, o_ref):
    s = jax.nn.silu(c_ref[...])
    o_ref[0] = jnp.dot(s, w_ref[0], preferred_element_type=F32,
                       precision=lax.Precision.HIGHEST) + b_ref[0]


def _modulation(c_all, w_mod, b_mod):
    depth = w_mod.shape[0]
    n_tiles = 6
    return pl.pallas_call(
        _mod_kernel,
        grid=(depth, n_tiles),
        in_specs=[
            pl.BlockSpec((MOD_ROWS, D_MODEL), lambda i, j: (0, 0)),
            pl.BlockSpec((1, D_MODEL, D_MODEL), lambda i, j: (i, 0, j)),
            pl.BlockSpec((1, 1, D_MODEL), lambda i, j: (i, 0, j)),
        ],
        out_specs=pl.BlockSpec((1, MOD_ROWS, D_MODEL), lambda i, j: (i, 0, j)),
        out_shape=jax.ShapeDtypeStruct((depth, MOD_ROWS, 6 * D_MODEL), F32),
        name="modulation",
    )(c_all, w_mod, b_mod.reshape(depth, 1, 6 * D_MODEL))


def _proj_kernel(*refs, n_chunks, add_pe, ssm_only):
    it = iter(refs)
    x_ref = next(it)
    pe_ref = next(it) if add_pe else None
    mod_ref, nw_ref, win_ref = next(it), next(it), next(it)
    if not ssm_only:
        sguw_ref, sgub_ref, band_ref, icnt_ref, poolw_ref, pscale_ref = (next(it) for _ in range(6))
    u_ref = next(it)
    cat_ref = None if ssm_only else next(it)

    tt = n_chunks * CHUNK
    x = x_ref[0]
    if add_pe:
        x = x + pe_ref[...]
    h = _rms(x, nw_ref[...] * (1.0 + mod_ref[1, 0])) + mod_ref[0, 0]
    hb = h.astype(BF16)

    zs = _dot_nt(win_ref[S_LO:P_LO, :], hb)
    for r in range(n_chunks):
        u_ref[:, r, :, :] = zs[:, r * CHUNK:(r + 1) * CHUNK].reshape(SSM_GROUPS, SSM_GROUP, CHUNK)
    if ssm_only:
        return

    za = jax.nn.gelu(_dot_nt(win_ref[0:S_LO, :], hb))
    v = za[A_WIDTH:, :].reshape(A_HEADS, A_HEAD_DIM, tt)
    mu = jnp.mean(v, axis=1, keepdims=True)
    vc = v - mu
    var = jnp.mean(vc * vc, axis=1, keepdims=True)
    vn = (vc * lax.rsqrt(var + EPS)).astype(BF16)
    for hd in range(A_HEADS):
        lhs = jnp.concatenate([vn[hd][:, r * CHUNK:(r + 1) * CHUNK] for r in range(n_chunks)], axis=0)
        s = _dot(lhs, sguw_ref[hd]) + sgub_ref[hd]
        for r in range(n_chunks):
            ug = za[hd * A_HEAD_DIM:(hd + 1) * A_HEAD_DIM, r * CHUNK:(r + 1) * CHUNK]
            a = ug * s[r * A_HEAD_DIM:(r + 1) * A_HEAD_DIM, :]
            cat_ref[0, r, hd * A_HEAD_DIM:(hd + 1) * A_HEAD_DIM, :] = a.astype(BF16)

    zp = _dot_nt(win_ref[P_LO:D_IN, :], hb)
    n_seg = tt // POOL_SEG
    diffs = []
    for i in range(len(POOL_WINDOWS)):
        pg = zp[i * POOL_GROUP:(i + 1) * POOL_GROUP, :]
        lhs = jnp.concatenate([pg[:, j * POOL_SEG:(j + 1) * POOL_SEG] for j in range(n_seg)], axis=0)
        m = _dot(lhs.astype(BF16), band_ref[i]) * icnt_ref[i]
        m = jnp.concatenate([m[j * POOL_GROUP:(j + 1) * POOL_GROUP, :] for j in range(n_seg)], axis=1)
        diffs.append(m - pg)
    dt = jnp.concatenate(diffs, axis=0).astype(BF16)
    pool = _dot(poolw_ref[...], dt) * pscale_ref[...]
    for r in range(n_chunks):
        cat_ref[0, r, A_WIDTH:, :] = pool[:, r * CHUNK:(r + 1) * CHUNK].astype(BF16)


def _proj_call(x, pe, mod_all, layer, mod_row, prep, pool_tables, *, n_chunks, ssm_only):
    bsz, seq, _ = x.shape
    tt = n_chunks * CHUNK
    chunks = seq // CHUNK
    add_pe = pe is not None

    in_specs = [pl.BlockSpec((1, tt, D_MODEL), lambda b, c: (b, c, 0))]
    args = [x]
    if add_pe:
        in_specs.append(pl.BlockSpec((tt, D_MODEL), lambda b, c: (c, 0)))
        args.append(pe)
    in_specs += [_mod_spec(layer, mod_row), _const_spec((1, D_MODEL), layer), _const_spec((D_IN, D_MODEL), layer)]
    args += [mod_all, prep["nw_mix_pre"], prep["win_t"]]
    if not ssm_only:
        band, icnt = pool_tables
        in_specs += [
            _const_spec((A_HEADS, CHUNK, CHUNK), layer),
            _const_spec((A_HEADS, 1, CHUNK), layer),
            _const_spec((len(POOL_WINDOWS), POOL_SEG, POOL_SEG)),
            _const_spec((len(POOL_WINDOWS), 1, POOL_SEG)),
            _const_spec((C_WIDTH, C_WIDTH), layer),
            _const_spec((C_WIDTH, 1), layer),
        ]
        args += [prep["sguw_t"], prep["sgub"], band, icnt, prep["poolw_t"], prep["pscale"]]

    u_shape = jax.ShapeDtypeStruct((SSM_GROUPS, chunks, bsz * SSM_GROUP, CHUNK), F32)
    u_spec = pl.BlockSpec((SSM_GROUPS, n_chunks, SSM_GROUP, CHUNK), lambda b, c: (0, c, b, 0))
    if ssm_only:
        out_shape, out_specs = u_shape, u_spec
    else:
        out_shape = (u_shape, jax.ShapeDtypeStruct((bsz, chunks, 2 * A_WIDTH, CHUNK), BF16))
        out_specs = (u_spec, pl.BlockSpec((1, n_chunks, 2 * A_WIDTH, CHUNK), lambda b, c: (b, c, 0, 0)))

    return pl.pallas_call(
        functools.partial(_proj_kernel, n_chunks=n_chunks, add_pe=add_pe, ssm_only=ssm_only),
        grid=(bsz, seq // tt),
        in_specs=in_specs,
        out_specs=out_specs,
        out_shape=out_shape,
        compiler_params=pltpu.CompilerParams(
            dimension_semantics=("parallel", "parallel"), vmem_limit_bytes=V7X_VMEM_LIMIT),
        name="proj_ssm_only" if ssm_only else "proj_local_mix",
    )(*args)


def _toeplitz_block(lag_ref, k_ref, hh, oo, rows):
    table = jnp.broadcast_to(lag_ref[0, hh, oo:oo + 1, :], (CHUNK, 2 * CHUNK))
    shifted = pltpu.roll(table, 0, 1, stride=1, stride_axis=0)
    k_ref[rows, oo * CHUNK:(oo + 1) * CHUNK] = shifted[:, CHUNK:].astype(BF16)


def _ssm_group(ul_ref, uc_ref, lag_next_ref, pe_ref, pm_ref, bb_ref, cc_ref, lam_ref, d_ref, yl_ref, yc_ref,
               k_now_ref, k_next_ref, e_ref, min_ref, lhs_ref, sin_ref, *, rows_lat, rows_ctx, bsz):
    n_lat = rows_lat // bsz
    n_ctx = rows_ctx // bsz
    half = 2 * SSM_STATE

    p_re, p_im = pe_ref[0, 0], pe_ref[0, 1]
    q_re, q_im = pm_ref[0, 0], pm_ref[0, 1]
    for hh in range(SSM_GROUP):
        rows = slice(hh * CHUNK, (hh + 1) * CHUNK)
        b_re, b_im = bb_ref[0, 0, hh:hh + 1, :], bb_ref[0, 1, hh:hh + 1, :]
        e_ref[rows, 0:half] = (b_re * p_re - b_im * p_im).astype(BF16)
        e_ref[rows, half:] = (b_re * p_im + b_im * p_re).astype(BF16)
        c_re, c_im = cc_ref[0, 0, :, hh:hh + 1], cc_ref[0, 1, :, hh:hh + 1]
        min_ref[0:half, rows] = (c_re * q_re - c_im * q_im).astype(BF16)
        min_ref[half:, rows] = (-(c_re * q_im + c_im * q_re)).astype(BF16)

    for hh in range(SSM_GROUP):
        lanes = slice(hh * CHUNK, (hh + 1) * CHUNK)
        lhs_ref[0:rows_lat, lanes] = ul_ref[0, pl.ds(hh, rows_lat, stride=SSM_GROUP), :].astype(BF16)
        lhs_ref[rows_lat:, lanes] = uc_ref[0, pl.ds(hh, rows_ctx, stride=SSM_GROUP), :].astype(BF16)
    sloc = _dot(lhs_ref[...], e_ref[...])
    lam_re = lam_ref[0, :, 0:half]
    lam_im = lam_ref[0, :, half:]
    fwd_lane = lax.broadcasted_iota(jnp.int32, (bsz, half), 1) < SSM_STATE

    def step(state, add):
        s_re, s_im = state
        a_re, a_im = add
        return (lam_re * s_re - lam_im * s_im + a_re, lam_re * s_im + lam_im * s_re + a_im)

    def rows_of(base, c):
        blk = sloc[base + c * bsz:base + (c + 1) * bsz, :]
        return blk[:, 0:half], blk[:, half:]

    def put(base, c, f_state, b_state):
        sin_ref[base + c * bsz:base + (c + 1) * bsz, 0:half] = jnp.where(
            fwd_lane, f_state[0], b_state[0]).astype(BF16)
        sin_ref[base + c * bsz:base + (c + 1) * bsz, half:] = jnp.where(
            fwd_lane, f_state[1], b_state[1]).astype(BF16)

    def sweep(base, count, f0, b0):
        f_in, b_in = [None] * count, [None] * count
        f, b = f0, b0
        for c in range(count):
            f_in[c] = f
            f = step(f, rows_of(base, c))
        f_end = f
        for c in reversed(range(count)):
            b_in[c] = b
            b = step(b, rows_of(base, c))
        for c in range(count):
            put(base, c, f_in[c], b_in[c])
        return f_end, b

    zero = (jnp.zeros((bsz, half), F32), jnp.zeros((bsz, half), F32))
    f_end, b_end = sweep(rows_lat, n_ctx, zero, zero)
    sweep(0, n_lat, f_end, b_end)

    pair = 2 * CHUNK
    out_rows = rows_lat if yc_ref is None else rows_lat + rows_ctx
    for q in range(SSM_WIDE // pair):
        cols = slice(q * pair, (q + 1) * pair)
        y = _dot(lhs_ref[0:out_rows, :], k_now_ref[:, cols]) + _dot(sin_ref[0:out_rows, :], min_ref[:, cols])
        for k in range(2):
            hh = 2 * q + k
            piece = y[:, k * CHUNK:(k + 1) * CHUNK]
            skip = d_ref[0, hh:hh + 1, :]
            lat_rows = pl.ds(hh, rows_lat, stride=SSM_GROUP)
            yl_ref[0, lat_rows, :] = jax.nn.gelu(piece[0:rows_lat] + skip * ul_ref[0, lat_rows, :])
            if yc_ref is not None:
                ctx_rows = pl.ds(hh, rows_ctx, stride=SSM_GROUP)
                yc_ref[0, ctx_rows, :] = jax.nn.gelu(piece[rows_lat:] + skip * uc_ref[0, ctx_rows, :])
            for oo in range(SSM_GROUP):
                _toeplitz_block(lag_next_ref, k_next_ref, hh, oo, slice(hh * CHUNK, (hh + 1) * CHUNK))


def _ssm_kernel(*refs, rows_lat, rows_ctx, bsz, ctx_out):
    it = iter(refs)
    ul_ref, uc_ref, lag_now_ref, lag_next_ref, pe_ref, pm_ref, bb_ref, cc_ref, lam_ref, d_ref, yl_ref = (
        next(it) for _ in range(11))
    yc_ref = next(it) if ctx_out else None
    k_even_ref, k_odd_ref, e_ref, min_ref, lhs_ref, sin_ref = (next(it) for _ in range(6))
    g = pl.program_id(0)

    @pl.when(g == 0)
    def _():
        def per_in_channel(hh, carry):
            rows = pl.ds(pl.multiple_of(hh * CHUNK, CHUNK), CHUNK)
            for oo in range(SSM_GROUP):
                _toeplitz_block(lag_now_ref, k_even_ref, hh, oo, rows)
            return carry

        lax.fori_loop(0, SSM_GROUP, per_in_channel, 0)

    group = functools.partial(_ssm_group, ul_ref, uc_ref, lag_next_ref, pe_ref, pm_ref, bb_ref, cc_ref, lam_ref,
                              d_ref, yl_ref, yc_ref, e_ref=e_ref, min_ref=min_ref, lhs_ref=lhs_ref,
                              sin_ref=sin_ref, rows_lat=rows_lat, rows_ctx=rows_ctx, bsz=bsz)

    @pl.when(g % 2 == 0)
    def _():
        group(k_now_ref=k_even_ref, k_next_ref=k_odd_ref)

    @pl.when(g % 2 == 1)
    def _():
        group(k_now_ref=k_odd_ref, k_next_ref=k_even_ref)


def _ssm_call(u_lat, u_ctx, ops, layer, *, ctx_out):
    groups, c_lat = u_lat.shape[:2]
    c_ctx = u_ctx.shape[1]
    bsz = u_lat.shape[2] // SSM_GROUP
    rows_lat, rows_ctx = c_lat * bsz, c_ctx * bsz
    rows = rows_lat + rows_ctx
    off = layer * groups
    state_w = 4 * SSM_STATE
    ul = u_lat.reshape(groups, rows_lat * SSM_GROUP, CHUNK)
    uc = u_ctx.reshape(groups, rows_ctx * SSM_GROUP, CHUNK)
    per_g = lambda g: (g, 0, 0)
    per_lg3 = lambda g: (g + off, 0, 0)
    per_lg4 = lambda g: (g + off, 0, 0, 0)
    next_lg4 = lambda g: (jnp.minimum(g + 1, groups - 1) + off, 0, 0, 0)
    out_shape = [jax.ShapeDtypeStruct(ul.shape, F32)]
    out_specs = [pl.BlockSpec((1, rows_lat * SSM_GROUP, CHUNK), per_g)]
    if ctx_out:
        out_shape.append(jax.ShapeDtypeStruct(uc.shape, F32))
        out_specs.append(pl.BlockSpec((1, rows_ctx * SSM_GROUP, CHUNK), per_g))
    outs = pl.pallas_call(
        functools.partial(_ssm_kernel, rows_lat=rows_lat, rows_ctx=rows_ctx, bsz=bsz, ctx_out=ctx_out),
        grid=(groups,),
        in_specs=[
            pl.BlockSpec((1, rows_lat * SSM_GROUP, CHUNK), per_g),
            pl.BlockSpec((1, rows_ctx * SSM_GROUP, CHUNK), per_g),
            pl.BlockSpec((1, SSM_GROUP, SSM_GROUP, 2 * CHUNK), per_lg4),
            pl.BlockSpec((1, SSM_GROUP, SSM_GROUP, 2 * CHUNK), next_lg4),
            pl.BlockSpec((1, 2, CHUNK, 2 * SSM_STATE), per_lg4),
            pl.BlockSpec((1, 2, 2 * SSM_STATE, CHUNK), per_lg4),
            pl.BlockSpec((1, 2, SSM_GROUP, 2 * SSM_STATE), per_lg4),
            pl.BlockSpec((1, 2, 2 * SSM_STATE, SSM_GROUP), per_lg4),
            pl.BlockSpec((1, 1, state_w), per_lg3),
            pl.BlockSpec((1, SSM_GROUP, CHUNK), per_lg3),
        ],
        out_specs=out_specs,
        out_shape=out_shape,
        scratch_shapes=[
            pltpu.VMEM((SSM_WIDE, SSM_WIDE), BF16), pltpu.VMEM((SSM_WIDE, SSM_WIDE), BF16),
            pltpu.VMEM((SSM_WIDE, state_w), BF16), pltpu.VMEM((state_w, SSM_WIDE), BF16),
            pltpu.VMEM((rows, SSM_WIDE), BF16), pltpu.VMEM((rows, state_w), BF16),
        ],
        compiler_params=pltpu.CompilerParams(
            dimension_semantics=("arbitrary",), vmem_limit_bytes=V7X_VMEM_LIMIT),
        name="ssm_chunked_scan",
    )(ul, uc, ops["lag_table"], ops["lag_table"], ops["pow_src"], ops["pow_out"], ops["b_bar"], ops["c_out"],
      ops["lam_t"], ops["d_rows"])
    y_lat = outs[0].reshape(u_lat.shape)
    y_ctx = outs[1].reshape(u_ctx.shape) if ctx_out else None
    return y_lat, y_ctx


def _mix_ffn_kernel(*refs, n_chunks, add_pe):
    it = iter(refs)
    x_ref = next(it)
    pe_ref = next(it) if add_pe else None
    (y_ref, cat_ref, mod_ref, nw_mix_ref, nw_pre_ref, nw_post_ref, gluw_ref, glub_ref, wout_ref,
     wg_ref, wu_ref, wd_ref, o_ref, x1_ref, acc_ref) = (next(it) for _ in range(15))

    g = jnp.concatenate([y_ref[:, r, :, :].reshape(B_WIDTH, CHUNK) for r in range(n_chunks)], axis=1)
    gate = jax.nn.sigmoid(_dot(gluw_ref[...], g.astype(BF16)) + glub_ref[...])
    st = (g * gate).astype(BF16)
    at = jnp.concatenate([cat_ref[0, r, 0:A_WIDTH, :] for r in range(n_chunks)], axis=1)
    pt = jnp.concatenate([cat_ref[0, r, A_WIDTH:, :] for r in range(n_chunks)], axis=1)
    cat_t = jnp.concatenate([at, st, pt], axis=0)
    m = _dot_tn(cat_t, wout_ref[...])
    x = x_ref[0]
    if add_pe:
        x = x + pe_ref[...]
    x1 = x + _rms(m, nw_mix_ref[...] * mod_ref[2, 0])
    x1_ref[...] = x1

    h = (_rms(x1, nw_pre_ref[...] * (1.0 + mod_ref[4, 0])) + mod_ref[3, 0]).astype(BF16)
    for k in range(N_FF_CHUNKS):
        cols = slice(k * FF_CHUNK, (k + 1) * FF_CHUNK)
        act = (jax.nn.silu(_dot(h, wg_ref[:, cols])) * _dot(h, wu_ref[:, cols])).astype(BF16)
        part = _dot(act, wd_ref[cols, :])
        if k == 0:
            acc_ref[...] = part
        else:
            acc_ref[...] += part
    o_ref[0] = x1_ref[...] + _rms(acc_ref[...], nw_post_ref[...] * mod_ref[5, 0])


def _mix_ffn_call(x, pe, y, cat, mod_all, layer, mod_row, prep, *, n_chunks):
    bsz, seq, _ = x.shape
    tt = n_chunks * CHUNK
    add_pe = pe is not None
    in_specs = [pl.BlockSpec((1, tt, D_MODEL), lambda b, c: (b, c, 0))]
    args = [x]
    if add_pe:
        in_specs.append(pl.BlockSpec((tt, D_MODEL), lambda b, c: (c, 0)))
        args.append(pe)
    in_specs += [
        pl.BlockSpec((SSM_GROUPS, n_chunks, SSM_GROUP, CHUNK), lambda b, c: (0, c, b, 0)),
        pl.BlockSpec((1, n_chunks, 2 * A_WIDTH, CHUNK), lambda b, c: (b, c, 0, 0)),
        _mod_spec(layer, mod_row),
        _const_spec((1, D_MODEL), layer),
        _const_spec((1, D_MODEL), layer),
        _const_spec((1, D_MODEL), layer),
        _const_spec((B_WIDTH, B_WIDTH), layer),
        _const_spec((B_WIDTH, 1), layer),
        _const_spec((D_MODEL, D_MODEL), layer),
        _const_spec((D_MODEL, D_FF), layer),
        _const_spec((D_MODEL, D_FF), layer),
        _const_spec((D_FF, D_MODEL), layer),
    ]
    args += [y, cat, mod_all, prep["nw_mix_post"], prep["nw_ffn_pre"], prep["nw_ffn_post"], prep["gluw_t"],
             prep["glub_col"], prep["wout"], prep["wg"], prep["wu"], prep["wd"]]
    return pl.pallas_call(
        functools.partial(_mix_ffn_kernel, n_chunks=n_chunks, add_pe=add_pe),
        grid=(bsz, seq // tt),
        in_specs=in_specs,
        out_specs=pl.BlockSpec((1, tt, D_MODEL), lambda b, c: (b, c, 0)),
        out_shape=jax.ShapeDtypeStruct(x.shape, F32),
        scratch_shapes=[pltpu.VMEM((tt, D_MODEL), F32), pltpu.VMEM((tt, D_MODEL), F32)],
        compiler_params=pltpu.CompilerParams(
            dimension_semantics=("parallel", "parallel"), vmem_limit_bytes=V7X_VMEM_LIMIT),
        name="mix_out_ffn",
    )(*args)


def _sincos_2d(rows, cols, dim):
    quarter = dim // 4
    f32 = np.float32
    omega = (f32(1.0) / (f32(10000.0) ** (np.arange(quarter, dtype=f32) / f32(quarter)))).astype(f32)
    r = (np.arange(rows, dtype=f32)[:, None] * omega).astype(np.float64)
    cc = (np.arange(cols, dtype=f32)[:, None] * omega).astype(np.float64)
    er = np.concatenate([np.sin(r), np.cos(r)], axis=-1)
    ec = np.concatenate([np.sin(cc), np.cos(cc)], axis=-1)
    pe = np.concatenate([np.broadcast_to(er[:, None, :], (rows, cols, dim // 2)),
                         np.broadcast_to(ec[None, :, :], (rows, cols, dim // 2))], axis=-1)
    return jnp.asarray(pe.reshape(rows * cols, dim), F32)


def _pool_bands(row_len):
    pos = np.arange(POOL_SEG)
    start = (pos // row_len) * row_len
    t = pos - start
    bands, icnts = [], []
    for w in POOL_WINDOWS:
        lo = np.clip(t - w // 2, 0, row_len) + start
        hi = np.clip(t - w // 2 + w, 0, row_len) + start
        bands.append(((pos[:, None] >= lo[None, :]) & (pos[:, None] < hi[None, :])).astype(np.float32))
        icnts.append((1.0 / (hi - lo).astype(np.float32))[None, :])
    return jnp.asarray(np.stack(bands), BF16), jnp.asarray(np.stack(icnts), F32)


def _ssm_operators(lam_re, lam_im, log_dt, b_re, b_im, c_re, c_im, d):
    t_len = CHUNK
    groups = lam_re.shape[1]
    dt = jnp.exp(log_dt)[..., None]
    a_re, a_im = lam_re * dt, lam_im * dt
    mag = jnp.exp(a_re)
    lb_re, lb_im = mag * jnp.cos(a_im), mag * jnp.sin(a_im)
    den = lam_re * lam_re + lam_im * lam_im
    q_re = ((lb_re - 1.0) * lam_re + lb_im * lam_im) / den
    q_im = (lb_im * lam_re - (lb_re - 1.0) * lam_im) / den
    bb_re = q_re[..., None] * b_re - q_im[..., None] * b_im
    bb_im = q_re[..., None] * b_im + q_im[..., None] * b_re
    k = jnp.arange(t_len + 1, dtype=F32)
    p_mag = jnp.exp(a_re[..., None] * k)
    p_re = p_mag * jnp.cos(a_im[..., None] * k)
    p_im = p_mag * jnp.sin(a_im[..., None] * k)

    cb_re = c_re[:, :, None] * jnp.swapaxes(bb_re, 2, 3)[:, :, :, None] \
        - c_im[:, :, None] * jnp.swapaxes(bb_im, 2, 3)[:, :, :, None]
    cb_im = c_re[:, :, None] * jnp.swapaxes(bb_im, 2, 3)[:, :, :, None] \
        + c_im[:, :, None] * jnp.swapaxes(bb_re, 2, 3)[:, :, :, None]
    hp = lax.Precision.HIGHEST
    resp = (jnp.einsum('dghop,dgpk->dghok', cb_re, p_re[..., :t_len], precision=hp)
            - jnp.einsum('dghop,dgpk->dghok', cb_im, p_im[..., :t_len], precision=hp))
    r_f, r_b = resp[0], resp[1]
    lag_table = jnp.concatenate([jnp.zeros_like(r_f[..., :1]), r_b[..., :0:-1],
                                 r_f[..., :1] + r_b[..., :1], r_f[..., 1:]], axis=-1)

    def both_dirs(fwd, bwd, axis):
        return jnp.concatenate([fwd, bwd], axis=axis)

    def re_im(re, im):
        return jnp.stack([re, im], axis=1)

    pow_src = re_im(jnp.swapaxes(both_dirs(p_re[0][..., t_len - 1::-1], p_re[1][..., :t_len], 1), 1, 2),
                    jnp.swapaxes(both_dirs(p_im[0][..., t_len - 1::-1], p_im[1][..., :t_len], 1), 1, 2))
    pow_out = re_im(both_dirs(p_re[0][..., 1:], p_re[1][..., :0:-1], 1),
                    both_dirs(p_im[0][..., 1:], p_im[1][..., :0:-1], 1))
    b_bar = re_im(jnp.swapaxes(both_dirs(bb_re[0], bb_re[1], 1), 1, 2),
                  jnp.swapaxes(both_dirs(bb_im[0], bb_im[1], 1), 1, 2))
    c_out = re_im(jnp.swapaxes(both_dirs(c_re[0], c_re[1], 2), 1, 2),
                  jnp.swapaxes(both_dirs(c_im[0], c_im[1], 2), 1, 2))
    lam_t = jnp.concatenate([p_re[0][..., t_len], p_re[1][..., t_len],
                             p_im[0][..., t_len], p_im[1][..., t_len]], axis=-1)[:, None, :]
    d_rows = jnp.broadcast_to(d[:, :, None], (groups, SSM_GROUP, CHUNK))
    return {"lag_table": lag_table, "pow_src": pow_src, "pow_out": pow_out, "b_bar": b_bar, "c_out": c_out,
            "lam_t": lam_t, "d_rows": d_rows}


def _fold_layers(p):
    p = jnp.swapaxes(p, 0, 1)
    return p.reshape((2, p.shape[1] * p.shape[2]) + p.shape[3:])


def _block_diag_t(blocks):
    depth, n, k, _ = blocks.shape
    eye = jnp.eye(n, dtype=blocks.dtype)
    out = jnp.einsum('dnab,nm->dnbma', blocks, eye)
    return out.reshape(depth, n * k, n * k)


def kernel(x, c, ctx, c_ctx, w_mod, b_mod, norm_mix_pre, norm_mix_post, norm_ffn_pre, norm_ffn_post, w_in, w_out, sgu_w, sgu_b, ssm_lam_re, ssm_lam_im, ssm_log_dt, ssm_b_re, ssm_b_im, ssm_c_re, ssm_c_im, ssm_d, glu_w, glu_b, pool_w, pool_scale, ffn_w_gate, ffn_w_up, ffn_w_down):
    bsz, n_lat, _ = x.shape
    n_ctx = ctx.shape[1]
    depth = w_mod.shape[0]
    assert bsz + 1 <= MOD_ROWS and n_lat % (8 * CHUNK) == 0 and n_ctx == POOL_SEG

    pe = _sincos_2d(n_lat // GRID_W, GRID_W, D_MODEL)
    c_all = jnp.concatenate([c, c_ctx[None, :], jnp.zeros((MOD_ROWS - bsz - 1, D_MODEL), F32)], axis=0)
    mod_all = _modulation(c_all, w_mod, b_mod)
    mod_all = jnp.transpose(mod_all.reshape(depth, MOD_ROWS, 6, 1, D_MODEL), (0, 2, 1, 3, 4))
    ctx_row = bsz

    band_lat = _pool_bands(GRID_W)
    band_ctx = _pool_bands(n_ctx)

    prep = {
        "nw_mix_pre": norm_mix_pre[:, None, :], "nw_mix_post": norm_mix_post[:, None, :],
        "nw_ffn_pre": norm_ffn_pre[:, None, :], "nw_ffn_post": norm_ffn_post[:, None, :],
        "win_t": jnp.swapaxes(w_in, 1, 2).astype(BF16),
        "sguw_t": jnp.swapaxes(sgu_w, 2, 3).astype(BF16),
        "sgub": sgu_b[:, :, None, :],
        "poolw_t": _block_diag_t(pool_w).astype(BF16),
        "pscale": pool_scale[:, :, None],
        "gluw_t": jnp.swapaxes(glu_w, 1, 2).astype(BF16),
        "glub_col": glu_b[:, :, None],
        "wout": w_out.astype(BF16),
        "wg": ffn_w_gate.astype(BF16),
        "wu": ffn_w_up.astype(BF16),
        "wd": ffn_w_down.astype(BF16),
    }
    ops = _ssm_operators(_fold_layers(ssm_lam_re), _fold_layers(ssm_lam_im), _fold_layers(ssm_log_dt),
                         _fold_layers(ssm_b_re), _fold_layers(ssm_b_im), _fold_layers(ssm_c_re),
                         _fold_layers(ssm_c_im), ssm_d.reshape(depth * SSM_GROUPS, SSM_GROUP))

    x_lat, x_ctx = x, ctx
    for i in range(depth):
        need_ctx = i < depth - 1
        pe_i = pe if i == 0 else None
        u_lat, cat_lat = _proj_call(x_lat, pe_i, mod_all, i, None, prep, band_lat, n_chunks=16, ssm_only=False)
        if need_ctx:
            u_ctx, cat_ctx = _proj_call(x_ctx, None, mod_all, i, ctx_row, prep, band_ctx, n_chunks=2, ssm_only=False)
        else:
            u_ctx = _proj_call(x_ctx, None, mod_all, i, ctx_row, prep, None, n_chunks=2, ssm_only=True)
        y_lat, y_ctx = _ssm_call(u_lat, u_ctx, ops, i, ctx_out=need_ctx)
        x_lat = _mix_ffn_call(x_lat, pe_i, y_lat, cat_lat, mod_all, i, None, prep, n_chunks=4)
        if need_ctx:
            x_ctx = _mix_ffn_call(x_ctx, None, y_ctx, cat_ctx, mod_all, i, ctx_row, prep, n_chunks=2)
    return x_lat
```
